```python
import jax, jax.numpy as jnp
from jax import lax
import numpy as np

D_MODEL = 1024
BATCH = 16
SEQ = 2048
DEPTH = 4
DEC_BATCH = 8
DEC_SEQ = 8192
PAST_LEN = 128

GRID_W = 64
NA_HEADS = 8
NA_HEAD_DIM = 64
NA_WIDTH = NA_HEADS * NA_HEAD_DIM
NA_KH_MAX = 8
NA_KW = 16
HG_HEADS = 4
HG_HEAD_DIM = 128
HG_WIDTH = HG_HEADS * HG_HEAD_DIM
HG_CHUNK = 64
N_EXPERTS = 16
EC_FACTOR = 2
D_FF = 2 * D_MODEL
IN_WIDTH = 3 * NA_WIDTH + 5 * HG_WIDTH + 2 * D_MODEL
ALPHA = (2 * DEPTH) ** 0.25
BETA = (8 * DEPTH) ** -0.25
LN_EPS = 1e-5
RMS_EPS = 1e-6
NEG_BIG = -1e30
F_MIN = 1e-30

kernel_name = 'hybrid_na_hgrn2_ec_encoder'


def layer_norm(x, g, b):
    xf = x.astype(jnp.float32)
    mu = jnp.mean(xf, axis=-1, keepdims=True)
    var = jnp.mean(jnp.square(xf - mu), axis=-1, keepdims=True)
    return ((xf - mu) * lax.rsqrt(var + LN_EPS)).astype(x.dtype) * g + b


def neighbourhood_attention(q, k, v, rpb):
    B, T, H, dh = q.shape
    rows = T // GRID_W
    kh = min(NA_KH_MAX, rows)
    r = jnp.arange(rows)
    row_idx = jnp.clip(r - kh // 2, 0, rows - kh)[:, None] + jnp.arange(kh)[None, :]
    c = jnp.arange(GRID_W)
    c_start = jnp.clip(c - NA_KW // 2, 0, GRID_W - NA_KW)
    col_mask = (c[None, :] >= c_start[:, None]) & (c[None, :] < c_start[:, None] + NA_KW)
    dr = row_idx - r[:, None] + (NA_KH_MAX - 1)
    dc = jnp.clip(c[None, :] - c[:, None], -(NA_KW - 1), NA_KW - 1) + (NA_KW - 1)
    bias = rpb[:, dr[:, None, :, None], dc[None, :, None, :]].astype(jnp.float32)
    bias = jnp.where(col_mask[None, None, :, None, :], bias, NEG_BIG)
    scale = dh ** -0.5

    def one(args):
        qb, kb, vb = args
        qg = qb.reshape(rows, GRID_W, H, dh)
        kg = kb.reshape(rows, GRID_W, H, dh)[row_idx]
        vg = vb.reshape(rows, GRID_W, H, dh)[row_idx]
        s = jnp.einsum('rqhd,rjkhd->hrqjk', qg, kg, preferred_element_type=jnp.float32) * scale + bias
        p = jax.nn.softmax(s.reshape(H, rows, GRID_W, kh * GRID_W), axis=-1)
        p = p.reshape(s.shape).astype(vb.dtype)
        o = jnp.einsum('hrqjk,rjkhd->rqhd', p, vg)
        return o.reshape(T, H, dh)

    return lax.map(one, (q, k, v))


def lower_bounds(p):
    sm = jax.nn.softmax(p.astype(jnp.float32), axis=0)
    return jnp.cumsum(sm, axis=0) - sm[0:1]


def forget_gates(z, lb):
    z = z.astype(jnp.float32)
    f = lb + (1.0 - lb) * jax.nn.sigmoid(z)
    log_f = jnp.log(jnp.maximum(f, F_MIN))
    k = (1.0 - lb) * jax.nn.sigmoid(-z)
    return k, log_f


def hgrn2_chunk_scan(q, k, v, log_f):
    B, T, H, dk = q.shape
    dv = v.shape[-1]
    n = T // HG_CHUNK

    def chunks(a):
        return a.reshape(B, n, HG_CHUNK, H, a.shape[-1]).transpose(1, 0, 3, 2, 4)

    tri = jnp.tril(jnp.ones((HG_CHUNK, HG_CHUNK), dtype=bool))

    def step(S, inp):
        qc, kc, vc, gc = inp
        b = jnp.cumsum(gc, axis=2)
        o_inter = jnp.einsum('bhtk,bhkv->bhtv', qc * jnp.exp(b), S)
        diff = jnp.where(tri[:, :, None], b[:, :, :, None, :] - b[:, :, None, :, :], NEG_BIG)
        a = jnp.einsum('bhtk,bhsk,bhtsk->bhts', qc, kc, jnp.exp(diff))
        o = o_inter + jnp.einsum('bhts,bhsv->bhtv', a, vc)
        b_last = b[:, :, -1]
        S = jnp.exp(b_last)[..., None] * S + jnp.einsum('bhsk,bhsv->bhkv', kc * jnp.exp(b_last[:, :, None] - b), vc)
        return S, o

    S0 = jnp.zeros((B, H, dk, dv), jnp.float32)
    _, o = lax.scan(step, S0, (chunks(q), chunks(k), chunks(v), chunks(log_f)))
    return o.transpose(1, 0, 3, 2, 4).reshape(B, T, H, dv)


def hgrn2_bidirectional(q, i, z_fwd, z_bwd, lb_fwd, lb_bwd):
    q = q.astype(jnp.float32)
    v = i.astype(jnp.float32)
    k_f, g_f = forget_gates(z_fwd, lb_fwd)
    k_b, g_b = forget_gates(z_bwd, lb_bwd)
    o_f = hgrn2_chunk_scan(q, k_f, v, g_f)
    flip = lambda a: jnp.flip(a, axis=1)
    o_b = flip(hgrn2_chunk_scan(flip(q), flip(k_b), flip(v), flip(g_b)))
    return o_f + o_b


def rms_norm_heads(o, g):
    B, T, H, dv = o.shape
    o = o * lax.rsqrt(jnp.mean(jnp.square(o), axis=-1, keepdims=True) + RMS_EPS)
    return o.reshape(B, T, H * dv) * g.astype(jnp.float32)


def expert_choice_ffn(x, w_router, w_gate, w_up, w_down):
    B, T, D = x.shape
    n_tok = B * T
    cap = EC_FACTOR * n_tok // N_EXPERTS
    xf = x.reshape(n_tok, D)
    aff = jax.nn.softmax((xf @ w_router).astype(jnp.float32), axis=-1)
    gate, idx = lax.top_k(aff.T, cap)
    xs = xf[idx]
    hdn = jax.nn.silu(jnp.einsum('ecd,edf->ecf', xs, w_gate)) * jnp.einsum('ecd,edf->ecf', xs, w_up)
    out = jnp.einsum('ecf,efd->ecd', hdn, w_down) * gate[..., None].astype(x.dtype)
    y = jnp.zeros_like(xf).at[idx.reshape(-1)].add(out.reshape(-1, D))
    return y.reshape(B, T, D)


def trunk(x, ln_in_g, ln_in_b, w_in, na_rpb, hg_lb_fwd, hg_lb_bwd, hg_norm_g,
          w_branch_na, w_branch_hg, w_out, ln1_g, ln1_b, w_router,
          w_gate_e, w_up_e, w_down_e, ln2_g, ln2_b):
    B, T, _ = x.shape
    lb_fwd = lower_bounds(hg_lb_fwd)
    lb_bwd = lower_bounds(hg_lb_bwd)
    sizes = [NA_WIDTH] * 3 + [HG_WIDTH] * 5 + [D_MODEL]
    splits = [int(s) for s in np.cumsum(sizes)]
    heads_na = lambda a: a.reshape(B, T, NA_HEADS, NA_HEAD_DIM)
    heads_hg = lambda a: a.reshape(B, T, HG_HEADS, HG_HEAD_DIM)
    x = layer_norm(x, ln_in_g, ln_in_b)
    for l in range(DEPTH):
        h = x @ w_in[l]
        q_na, k_na, v_na, q_hg, z_f, z_b, i_hg, g_hg, gate_na, gate_hg = jnp.split(h, splits, axis=-1)
        a_out = neighbourhood_attention(heads_na(q_na), heads_na(k_na), heads_na(v_na), na_rpb[l])
        a_out = a_out.reshape(B, T, NA_WIDTH) @ w_branch_na[l]
        o_hg = hgrn2_bidirectional(heads_hg(jax.nn.silu(q_hg)), heads_hg(i_hg), heads_hg(z_f), heads_hg(z_b),
                                   lb_fwd[l].reshape(HG_HEADS, HG_HEAD_DIM),
                                   lb_bwd[l].reshape(HG_HEADS, HG_HEAD_DIM))
        o_hg = rms_norm_heads(o_hg, hg_norm_g[l]).astype(x.dtype)
        r_out = (o_hg * jax.nn.silu(g_hg)) @ w_branch_hg[l]
        mix = jax.nn.sigmoid(gate_na) * a_out + jax.nn.sigmoid(gate_hg) * r_out
        x = layer_norm(ALPHA * x + mix @ w_out[l], ln1_g[l], ln1_b[l])
        y = expert_choice_ffn(x, w_router[l], w_gate_e[l], w_up_e[l], w_down_e[l])
        x = layer_norm(ALPHA * x + y, ln2_g[l], ln2_b[l])
    return x


def setup_inputs(seed: int = 0) -> dict:
    key = jax.random.key(seed)
    ks = jax.random.split(key, 21)
    nrm = lambda k, shape, s: jax.random.normal(k, shape, jnp.float32) * s
    return {
        'x_prompt': nrm(ks[0], (BATCH, SEQ, D_MODEL), 1.0),
        'x_sample': nrm(ks[1], (DEC_BATCH, DEC_SEQ, D_MODEL), 1.0),
        'ln_in_g': 1.0 + nrm(ks[2], (D_MODEL,), 0.02),
        'ln_in_b': nrm(ks[3], (D_MODEL,), 0.02),
        'w_in': nrm(ks[4], (DEPTH, D_MODEL, IN_WIDTH), D_MODEL ** -0.5),
        'na_rpb': nrm(ks[5], (DEPTH, NA_HEADS, 2 * NA_KH_MAX - 1, 2 * NA_KW - 1), 0.1),
        'hg_lb_fwd': nrm(ks[6], (DEPTH, HG_WIDTH), 0.1),
        'hg_lb_bwd': nrm(ks[7], (DEPTH, HG_WIDTH), 0.1),
        'hg_norm_g': 1.0 + nrm(ks[8], (DEPTH, HG_WIDTH), 0.02),
        'w_branch_na': nrm(ks[9], (DEPTH, NA_WIDTH, D_MODEL), NA_WIDTH ** -0.5),
        'w_branch_hg': nrm(ks[10], (DEPTH, HG_WIDTH, D_MODEL), HG_WIDTH ** -0.5),
        'w_out': nrm(ks[11], (DEPTH, D_MODEL, D_MODEL), BETA * D_MODEL ** -0.5),
        'ln1_g': 1.0 + nrm(ks[12], (DEPTH, D_MODEL), 0.02),
        'ln1_b': nrm(ks[13], (DEPTH, D_MODEL), 0.02),
        'w_router': nrm(ks[14], (DEPTH, D_MODEL, N_EXPERTS), D_MODEL ** -0.5),
        'w_gate_e': nrm(ks[15], (DEPTH, N_EXPERTS, D_MODEL, D_FF), D_MODEL ** -0.5),
        'w_up_e': nrm(ks[16], (DEPTH, N_EXPERTS, D_MODEL, D_FF), D_MODEL ** -0.5),
        'w_down_e': nrm(ks[17], (DEPTH, N_EXPERTS, D_FF, D_MODEL), BETA * D_FF ** -0.5),
        'ln2_g': 1.0 + nrm(ks[18], (DEPTH, D_MODEL), 0.02),
        'ln2_b': nrm(ks[19], (DEPTH, D_MODEL), 0.02),
    }


def reference(x_prompt, x_sample, ln_in_g, ln_in_b, w_in, na_rpb, hg_lb_fwd, hg_lb_bwd, hg_norm_g,
              w_branch_na, w_branch_hg, w_out, ln1_g, ln1_b, w_router,
              w_gate_e, w_up_e, w_down_e, ln2_g, ln2_b):
    y_prompt = trunk(x_prompt, ln_in_g, ln_in_b, w_in, na_rpb, hg_lb_fwd, hg_lb_bwd, hg_norm_g,
                     w_branch_na, w_branch_hg, w_out, ln1_g, ln1_b, w_router,
                     w_gate_e, w_up_e, w_down_e, ln2_g, ln2_b)
    y_sample = trunk(x_sample, ln_in_g, ln_in_b, w_in, na_rpb, hg_lb_fwd, hg_lb_bwd, hg_norm_g,
                     w_branch_na, w_branch_hg, w_out, ln1_g, ln1_b, w_router,
                     w_gate_e, w_up_e, w_down_e, ln2_g, ln2_b)
    return (y_prompt, y_sample)
```

```python
import functools

import jax
import jax.numpy as jnp
import numpy as np
from jax import lax
from jax.experimental import pallas as pl
from jax.experimental.pallas import tpu as pltpu

F32 = jnp.float32
BF16 = jnp.bfloat16

D_MODEL = 1024
DEPTH = 4
GRID_W = 64
NA_HEADS = 8
NA_HEAD_DIM = 64
NA_WIDTH = NA_HEADS * NA_HEAD_DIM
NA_KH = 8
NA_KW = 16
HG_HEADS = 4
HG_HEAD_DIM = 128
HG_WIDTH = HG_HEADS * HG_HEAD_DIM
HG_CHUNK = 64
N_EXPERTS = 16
EC_FACTOR = 2
D_FF = 2 * D_MODEL
IN_WIDTH = 3 * NA_WIDTH + 5 * HG_WIDTH + 2 * D_MODEL
ALPHA = (2 * DEPTH) ** 0.25
LN_EPS = 1e-5
RMS_EPS = 1e-6
NEG_BIG = -1e30
F_MIN = 1e-30

HB_WIDTH = IN_WIDTH - 3 * NA_WIDTH
HG_LEVELS = (32, 16, 8, 4, 2, 1)
VMEM_LIMIT = 48 * 1024 * 1024


def _cparams(*sem):
    return pltpu.CompilerParams(dimension_semantics=sem, vmem_limit_bytes=VMEM_LIMIT)


def _sigmoid(x):
    return 1.0 / (1.0 + jnp.exp(-x))


def _layer_norm(x, g, b):
    mu = jnp.mean(x, axis=-1, keepdims=True)
    xc = x - mu
    var = jnp.mean(xc * xc, axis=-1, keepdims=True)
    return xc * lax.rsqrt(var + LN_EPS) * g + b


def _dot(a, b):
    return jnp.dot(a, b, preferred_element_type=F32)


def _dot_nt(a, b):
    return lax.dot_general(a, b, (((1,), (1,)), ((), ())), preferred_element_type=F32)


def _dot_tn(a, b):
    return lax.dot_general(a, b, (((0,), (0,)), ((), ())), preferred_element_type=F32)


def _ln_kernel(x_ref, g_ref, b_ref, o_ref):
    o_ref[...] = _layer_norm(x_ref[...], g_ref[...], b_ref[...])


def _ln_call(x, g, b, tm=512):
    n, d = x.shape
    row = pl.BlockSpec((tm, d), lambda i: (i, 0))
    vec = pl.BlockSpec((1, d), lambda i: (0, 0))
    return pl.pallas_call(
        _ln_kernel, grid=(n // tm,), in_specs=[row, vec, vec], out_specs=row,
        out_shape=jax.ShapeDtypeStruct((n, d), F32), compiler_params=_cparams("parallel"),
        name="ln_in")(x, g.reshape(1, d), b.reshape(1, d))


def _ln2_kernel(x_ref, y_ref, g_ref, b_ref, o_ref):
    o_ref[...] = _layer_norm(ALPHA * x_ref[...] + y_ref[...], g_ref[...], b_ref[...])


def _ln2_call(x, y, g, b, tm=512):
    n, d = x.shape
    row = pl.BlockSpec((tm, d), lambda i: (i, 0))
    vec = pl.BlockSpec((1, d), lambda i: (0, 0))
    return pl.pallas_call(
        _ln2_kernel, grid=(n // tm,), in_specs=[row, row, vec, vec], out_specs=row,
        out_shape=jax.ShapeDtypeStruct((n, d), F32), compiler_params=_cparams("parallel"),
        name="ln_ffn")(x, y, g.reshape(1, d), b.reshape(1, d))


def _inproj_kernel(x_ref, w_ref, qkv_ref, hb_ref):
    xb = x_ref[...].astype(BF16)
    cw = 512
    for c in range(IN_WIDTH // cw):
        acc = _dot(xb, w_ref[:, c * cw:(c + 1) * cw])
        if c * cw < 3 * NA_WIDTH:
            if c * cw < NA_WIDTH:
                acc = acc * (NA_HEAD_DIM ** -0.5)
            qkv_ref[:, c * cw:(c + 1) * cw] = acc.astype(BF16)
        else:
            o = c * cw - 3 * NA_WIDTH
            hb_ref[:, o:o + cw] = acc


def _inproj_call(x, w, tm=256):
    n, d = x.shape
    return pl.pallas_call(
        _inproj_kernel, grid=(n // tm,),
        in_specs=[pl.BlockSpec((tm, d), lambda i: (i, 0)),
                  pl.BlockSpec((d, IN_WIDTH), lambda i: (0, 0), pipeline_mode=pl.Buffered(1))],
        out_specs=[pl.BlockSpec((tm, 3 * NA_WIDTH), lambda i: (i, 0)),
                   pl.BlockSpec((tm, HB_WIDTH), lambda i: (i, 0))],
        out_shape=[jax.ShapeDtypeStruct((n, 3 * NA_WIDTH), BF16),
                   jax.ShapeDtypeStruct((n, HB_WIDTH), F32)],
        compiler_params=_cparams("parallel"), name="in_proj")(x, w)


def _na_kernel(q_ref, k_ref, v_ref, bias_ref, o_ref):
    lane = lax.broadcasted_iota(jnp.int32, (1, 128), 1)
    lo = lane < NA_HEAD_DIM
    for p in range(NA_HEADS // 2):
        cs = slice(p * 128, (p + 1) * 128)
        q2 = q_ref[:, cs]
        k2 = k_ref[:, cs]
        v2 = v_ref[:, cs]
        outs = []
        for hh in range(2):
            qm = jnp.where(lo if hh == 0 else jnp.logical_not(lo), q2, jnp.zeros_like(q2))
            s = _dot_nt(qm, k2) + bias_ref[2 * p + hh]
            m = jnp.max(s, axis=-1, keepdims=True)
            e = jnp.exp(s - m)
            l = jnp.sum(e, axis=-1, keepdims=True)
            outs.append(_dot(e.astype(BF16), v2) / l)
        o_ref[:, cs] = jnp.where(lo, outs[0], outs[1]).astype(BF16)


def _na_call(qkv, bias, batch, rows):
    n = qkv.shape[0]
    assert rows >= NA_KH

    def start(r):
        return jnp.clip(r - NA_KH // 2, 0, rows - NA_KH)

    win = (pl.Element(NA_KH * GRID_W), pl.Element(NA_WIDTH))
    return pl.pallas_call(
        _na_kernel, grid=(batch, rows),
        in_specs=[pl.BlockSpec((GRID_W, NA_WIDTH), lambda b, r: (b * rows + r, 0)),
                  pl.BlockSpec(win, lambda b, r: ((b * rows + start(r)) * GRID_W, NA_WIDTH)),
                  pl.BlockSpec(win, lambda b, r: ((b * rows + start(r)) * GRID_W, 2 * NA_WIDTH)),
                  pl.BlockSpec((None, NA_HEADS, GRID_W, NA_KH * GRID_W),
                               lambda b, r: (r - start(r), 0, 0, 0))],
        out_specs=pl.BlockSpec((GRID_W, NA_WIDTH), lambda b, r: (b * rows + r, 0)),
        out_shape=jax.ShapeDtypeStruct((n, NA_WIDTH), BF16),
        compiler_params=_cparams("parallel", "arbitrary"), name="na_attn")(qkv, qkv, qkv, bias)


def _na_bias_tables(rpb):
    cls = np.arange(NA_KH)[:, None]
    j = np.arange(NA_KH)[None, :]
    dr = j - cls + (NA_KH - 1)
    c = np.arange(GRID_W)
    c_start = np.clip(c - NA_KW // 2, 0, GRID_W - NA_KW)
    col_mask = (c[None, :] >= c_start[:, None]) & (c[None, :] < c_start[:, None] + NA_KW)
    dc = np.clip(c[None, :] - c[:, None], -(NA_KW - 1), NA_KW - 1) + (NA_KW - 1)
    t = rpb[:, :, dr[:, :, None, None], dc[None, None, :, :]].astype(F32)
    t = jnp.where(col_mask[None, None, None, None], t, NEG_BIG)
    t = t.transpose(0, 2, 1, 4, 3, 5)
    return t.reshape(rpb.shape[0], NA_KH, NA_HEADS, GRID_W, NA_KH * GRID_W)


def _hg_constants(reverse):
    n = HG_CHUNK
    t = np.arange(n)
    if reverse:
        tri = (t[None, :] >= t[:, None])
    else:
        tri = (t[None, :] <= t[:, None])
    masks = []
    for c in HG_LEVELS:
        blk = t // (2 * c)
        upper = (t % (2 * c)) >= c
        same = blk[:, None] == blk[None, :]
        if reverse:
            m = same & (~upper)[:, None] & upper[None, :]
        else:
            m = same & upper[:, None] & (~upper)[None, :]
        masks.append(m)
    masks.append(np.eye(n, dtype=bool))
    return jnp.asarray(tri, BF16), jnp.asarray(np.stack(masks), F32)


def _hgrn_kernel(*refs, reverse, final):
    if final:
        q_ref, z_ref, v_ref, lb_ref, tri_ref, lm_ref, gate_ref, prev_ref, ng_ref, o_ref, st_ref = refs
    else:
        q_ref, z_ref, v_ref, lb_ref, tri_ref, lm_ref, o_ref, st_ref = refs
    n = HG_CHUNK
    w = HG_WIDTH

    @pl.when(pl.program_id(1) == 0)
    def _():
        st_ref[...] = jnp.zeros_like(st_ref)

    z = z_ref[...]
    lb = lb_ref[...]
    f = lb + (1.0 - lb) * _sigmoid(z)
    g = jnp.log(jnp.maximum(f, F_MIN))
    kk = (1.0 - lb) * _sigmoid(-z)
    qv = q_ref[...]
    qq = qv * _sigmoid(qv)
    v_b = v_ref[...].astype(BF16)

    g1 = g.astype(BF16)
    r1 = g - g1.astype(F32)
    g2 = r1.astype(BF16)
    g3 = (r1 - g2.astype(F32)).astype(BF16)
    tri = tri_ref[...]
    b = _dot(tri, g1) + _dot(tri, g2) + _dot(tri, g3)
    last = 0 if reverse else n - 1
    b_last = b[last:last + 1, :]
    e_last = jnp.exp(b_last)

    row = lax.broadcasted_iota(jnp.int32, (n, 1), 0)
    qs, ks = [], []
    for c in HG_LEVELS:
        ref_row = c if reverse else c - 1
        if c == 1:
            d = jnp.where((row % 2) == (0 if reverse else 1), g, 0.0)
        elif c == 2:
            b3 = b.reshape(n // 8, 8, w)
            sub = lax.broadcasted_iota(jnp.int32, (1, 8, 1), 1)
            r_lo = jnp.broadcast_to(b3[:, ref_row:ref_row + 1, :], b3.shape)
            r_hi = jnp.broadcast_to(b3[:, ref_row + 4:ref_row + 5, :], b3.shape)
            d = -jnp.abs(b3 - jnp.where(sub < 4, r_lo, r_hi)).reshape(n, w)
        else:
            b3 = b.reshape(n // (2 * c), 2 * c, w)
            d = -jnp.abs(b3 - jnp.broadcast_to(b3[:, ref_row:ref_row + 1, :], b3.shape)).reshape(n, w)
        e = jnp.exp(d)
        qs.append((qq * e).astype(BF16))
        ks.append((kk * e).astype(BF16))
    qs.append(qq.astype(BF16))
    ks.append(kk.astype(BF16))
    q_in = (qq * jnp.exp(b)).astype(BF16)
    k_out = (kk * jnp.exp(b_last - b)).astype(BF16)

    outs = []
    for h in range(HG_HEADS):
        hs = slice(h * HG_HEAD_DIM, (h + 1) * HG_HEAD_DIM)
        a = None
        for li in range(len(qs)):
            term = lm_ref[li] * _dot_nt(qs[li][:, hs], ks[li][:, hs])
            a = term if a is None else a + term
        st = st_ref[h]
        o = _dot(a.astype(BF16), v_b[:, hs]) + _dot_nt(q_in[:, hs], st.astype(BF16))
        st_ref[h] = st * e_last[:, hs] + _dot_tn(v_b[:, hs], k_out[:, hs])
        outs.append(o)

    if not final:
        for h in range(HG_HEADS):
            o_ref[:, h * HG_HEAD_DIM:(h + 1) * HG_HEAD_DIM] = outs[h]
    else:
        gv = gate_ref[...]
        gs = gv * _sigmoid(gv)
        for h in range(HG_HEADS):
            hs = slice(h * HG_HEAD_DIM, (h + 1) * HG_HEAD_DIM)
            o = outs[h] + prev_ref[:, hs]
            o = o * lax.rsqrt(jnp.mean(o * o, axis=-1, keepdims=True) + RMS_EPS) * ng_ref[:, hs]
            o_ref[:, hs] = (o * gs[:, hs]).astype(BF16)


def _hgrn_call(hb, lb, z_col, batch, seq, reverse, prev=None, norm_g=None):
    n = hb.shape[0]
    nc = seq // HG_CHUNK
    final = prev is not None
    tri, masks = _hg_constants(reverse)

    def tok(b, c):
        return b * nc + ((nc - 1 - c) if reverse else c)

    def col(j):
        return pl.BlockSpec((HG_CHUNK, HG_WIDTH), lambda b, c: (tok(b, c), j))

    vec = pl.BlockSpec((1, HG_WIDTH), lambda b, c: (0, 0))
    in_specs = [col(4), col(z_col), col(7), vec,
                pl.BlockSpec((HG_CHUNK, HG_CHUNK), lambda b, c: (0, 0)),
                pl.BlockSpec(masks.shape, lambda b, c: (0, 0, 0))]
    args = [hb, hb, hb, lb.reshape(1, HG_WIDTH), tri, masks]
    if final:
        in_specs += [col(8), pl.BlockSpec((HG_CHUNK, HG_WIDTH), lambda b, c: (tok(b, c), 0)), vec]
        args += [hb, prev, norm_g.reshape(1, HG_WIDTH)]
    return pl.pallas_call(
        functools.partial(_hgrn_kernel, reverse=reverse, final=final),
        grid=(batch, nc), in_specs=in_specs,
        out_specs=pl.BlockSpec((HG_CHUNK, HG_WIDTH), lambda b, c: (tok(b, c), 0)),
        out_shape=jax.ShapeDtypeStruct((n, HG_WIDTH), BF16 if final else F32),
        scratch_shapes=[pltpu.VMEM((HG_HEADS, HG_HEAD_DIM, HG_HEAD_DIM), F32)],
        compiler_params=_cparams("arbitrary", "arbitrary"),
        name="hgrn_bwd" if reverse else "hgrn_fwd")(*args)


def _mix_kernel(na_ref, hg_ref, gt_ref, x_ref, wna_ref, whg_ref, wout_ref, g_ref, b_ref, wr_ref,
                x1_ref, x1b_ref, aff_ref):
    a = _dot(na_ref[...], wna_ref[...])
    r = _dot(hg_ref[...], whg_ref[...])
    mix = _sigmoid(gt_ref[:, :D_MODEL]) * a + _sigmoid(gt_ref[:, D_MODEL:]) * r
    y = _dot(mix.astype(BF16), wout_ref[...])
    x1 = _layer_norm(ALPHA * x_ref[...] + y, g_ref[...], b_ref[...])
    x1_ref[...] = x1
    x1b_ref[...] = x1.astype(BF16)
    logits = lax.dot_general(wr_ref[...], x1, (((1,), (1,)), ((), ())),
                             precision=lax.Precision.HIGHEST, preferred_element_type=F32)
    e = jnp.exp(logits - jnp.max(logits, axis=0, keepdims=True))
    aff_ref[...] = e / jnp.sum(e, axis=0, keepdims=True)


def _mix_call(na, hg, hb, x, wna, whg, wout, g, b, wr_t, tm=256):
    n, d = x.shape
    const = lambda shape: pl.BlockSpec(shape, lambda i: (0,) * len(shape))
    return pl.pallas_call(
        _mix_kernel, grid=(n // tm,),
        in_specs=[pl.BlockSpec((tm, NA_WIDTH), lambda i: (i, 0)),
                  pl.BlockSpec((tm, HG_WIDTH), lambda i: (i, 0)),
                  pl.BlockSpec((tm, 2 * D_MODEL), lambda i: (i, 0)),
                  pl.BlockSpec((tm, d), lambda i: (i, 0)),
                  const((NA_WIDTH, d)), const((HG_WIDTH, d)), const((d, d)),
                  const((1, d)), const((1, d)), const((N_EXPERTS, d))],
        out_specs=[pl.BlockSpec((tm, d), lambda i: (i, 0)),
                   pl.BlockSpec((tm, d), lambda i: (i, 0)),
                   pl.BlockSpec((N_EXPERTS, tm), lambda i: (0, i))],
        out_shape=[jax.ShapeDtypeStruct((n, d), F32), jax.ShapeDtypeStruct((n, d), BF16),
                   jax.ShapeDtypeStruct((N_EXPERTS, n), F32)],
        compiler_params=_cparams("parallel"), name="mix_router")(
            na, hg, hb, x, wna, whg, wout, g.reshape(1, d), b.reshape(1, d), wr_t)


def _ffn_kernel(xs_ref, wg_ref, wu_ref, wd_ref, gate_ref, o_ref):
    k = pl.program_id(2)
    xs = xs_ref[...]
    hg = _dot(xs, wg_ref[...])
    hu = _dot(xs, wu_ref[...])
    part = _dot((hg * _sigmoid(hg) * hu).astype(BF16), wd_ref[...])

    @pl.when(k == 0)
    def _():
        o_ref[...] = part

    @pl.when(k > 0)
    def _():
        o_ref[...] += part

    @pl.when(k == pl.num_programs(2) - 1)
    def _():
        o_ref[...] = o_ref[...] * gate_ref[...]


def _ffn_call(xs, wg, wu, wd, gate, tc=1024, tf=512):
    e, c, d = xs.shape
    f = wg.shape[-1]
    tc = min(tc, c)
    return pl.pallas_call(
        _ffn_kernel, grid=(e, c // tc, f // tf),
        in_specs=[pl.BlockSpec((None, tc, d), lambda i, j, k: (i, j, 0)),
                  pl.BlockSpec((None, d, tf), lambda i, j, k: (i, 0, k)),
                  pl.BlockSpec((None, d, tf), lambda i, j, k: (i, 0, k)),
                  pl.BlockSpec((None, tf, d), lambda i, j, k: (i, k, 0)),
                  pl.BlockSpec((None, tc, 1), lambda i, j, k: (i, j, 0))],
        out_specs=pl.BlockSpec((None, tc, d), lambda i, j, k: (i, j, 0)),
        out_shape=jax.ShapeDtypeStruct((e, c, d), F32),
        compiler_params=_cparams("parallel", "parallel", "arbitrary"), name="expert_ffn")(
            xs, wg, wu, wd, gate.reshape(e, c, 1))


def _lower_bounds(p):
    sm = jax.nn.softmax(p.astype(F32), axis=0)
    return jnp.cumsum(sm, axis=0) - sm[0:1]


def _trunk(x, prm):
    batch, seq, d = x.shape
    n = batch * seq
    rows = seq // GRID_W
    cap = EC_FACTOR * n // N_EXPERTS
    x = _ln_call(x.reshape(n, d), prm["ln_in_g"], prm["ln_in_b"])
    for l in range(DEPTH):
        qkv, hb = _inproj_call(x, prm["w_in"][l])
        na = _na_call(qkv, prm["na_bias"][l], batch, rows)
        o_f = _hgrn_call(hb, prm["lb_fwd"][l], 5, batch, seq, reverse=False)
        hg = _hgrn_call(hb, prm["lb_bwd"][l], 6, batch, seq, reverse=True, prev=o_f,
                        norm_g=prm["hg_norm_g"][l])
        x1, x1b, aff_t = _mix_call(na, hg, hb, x, prm["w_branch_na"][l], prm["w_branch_hg"][l],
                                   prm["w_out"][l], prm["ln1_g"][l], prm["ln1_b"][l], prm["w_router_t"][l])
        gate, idx = lax.top_k(aff_t, cap)
        xs = x1b[idx]
        out = _ffn_call(xs, prm["w_gate_e"][l], prm["w_up_e"][l], prm["w_down_e"][l], gate)
        y = jnp.zeros((n, d), F32).at[idx.reshape(-1)].add(out.reshape(-1, d))
        x = _ln2_call(x1, y, prm["ln2_g"][l], prm["ln2_b"][l])
    return x.reshape(batch, seq, d)


@jax.jit
def kernel(x_prompt, x_sample, ln_in_g, ln_in_b, w_in, na_rpb, hg_lb_fwd, hg_lb_bwd, hg_norm_g,
           w_branch_na, w_branch_hg, w_out, ln1_g, ln1_b, w_router, w_gate_e, w_up_e, w_down_e,
           ln2_g, ln2_b):
    na_w, hg_w = 3 * NA_WIDTH, 5 * HG_WIDTH
    w_in_r = jnp.concatenate([w_in[..., :na_w], w_in[..., na_w + hg_w:], w_in[..., na_w:na_w + hg_w]],
                             axis=-1).astype(BF16)
    prm = dict(
        ln_in_g=ln_in_g, ln_in_b=ln_in_b, w_in=w_in_r, na_bias=_na_bias_tables(na_rpb),
        lb_fwd=_lower_bounds(hg_lb_fwd), lb_bwd=_lower_bounds(hg_lb_bwd), hg_norm_g=hg_norm_g,
        w_branch_na=w_branch_na.astype(BF16), w_branch_hg=w_branch_hg.astype(BF16),
        w_out=w_out.astype(BF16), ln1_g=ln1_g, ln1_b=ln1_b,
        w_router_t=jnp.swapaxes(w_router, 1, 2),
        w_gate_e=w_gate_e.astype(BF16), w_up_e=w_up_e.astype(BF16), w_down_e=w_down_e.astype(BF16),
        ln2_g=ln2_g, ln2_b=ln2_b)
    return _trunk(x_prompt, prm), _trunk(x_sample, prm)
```

```python
import functools

import jax
import jax.numpy as jnp
import numpy as np
from jax import lax
from jax.experimental import pallas as pl
from jax.experimental.pallas import tpu as pltpu

F32 = jnp.float32
BF16 = jnp.bfloat16

D_MODEL = 1024
DEPTH = 4
GRID_W = 64
NA_HEADS = 8
NA_HEAD_DIM = 64
NA_WIDTH = NA_HEADS * NA_HEAD_DIM
NA_KH = 8
NA_KW = 16
HG_HEADS = 4
HG_HEAD_DIM = 128
HG_WIDTH = HG_HEADS * HG_HEAD_DIM
HG_CHUNK = 64
N_EXPERTS = 16
EC_FACTOR = 2
D_FF = 2 * D_MODEL
IN_WIDTH = 3 * NA_WIDTH + 5 * HG_WIDTH + 2 * D_MODEL
ALPHA = (2 * DEPTH) ** 0.25
LN_EPS = 1e-5
RMS_EPS = 1e-6
NEG_BIG = -1e30
F_MIN = 1e-30

GATE_WIDTH = 2 * D_MODEL
HB_WIDTH = 5 * HG_WIDTH
HG_LEVELS = (32, 16, 8, 4, 2, 1)
VMEM_LIMIT = 48 * 1024 * 1024


def _cparams(*sem):
    return pltpu.CompilerParams(dimension_semantics=sem, vmem_limit_bytes=VMEM_LIMIT)


def _sigmoid(x):
    return 1.0 / (1.0 + jnp.exp(-x))


def _layer_norm(x, g, b):
    mu = jnp.mean(x, axis=-1, keepdims=True)
    xc = x - mu
    var = jnp.mean(xc * xc, axis=-1, keepdims=True)
    return xc * lax.rsqrt(var + LN_EPS) * g + b


def _dot(a, b):
    return jnp.dot(a, b, preferred_element_type=F32)


def _dot_nt(a, b):
    return lax.dot_general(a, b, (((1,), (1,)), ((), ())), preferred_element_type=F32)


def _dot_tn(a, b):
    return lax.dot_general(a, b, (((0,), (0,)), ((), ())), preferred_element_type=F32)


def _ln_kernel(x_ref, g_ref, b_ref, o_ref):
    o_ref[...] = _layer_norm(x_ref[...], g_ref[...], b_ref[...])


def _ln_call(x, g, b, tm=512):
    n, d = x.shape
    row = pl.BlockSpec((tm, d), lambda i: (i, 0))
    vec = pl.BlockSpec((1, d), lambda i: (0, 0))
    return pl.pallas_call(
        _ln_kernel, grid=(n // tm,), in_specs=[row, vec, vec], out_specs=row,
        out_shape=jax.ShapeDtypeStruct((n, d), F32), compiler_params=_cparams("parallel"),
        name="ln_in")(x, g.reshape(1, d), b.reshape(1, d))


def _ln2_kernel(x_ref, y_ref, g_ref, b_ref, o_ref):
    o_ref[...] = _layer_norm(ALPHA * x_ref[...] + y_ref[...], g_ref[...], b_ref[...])


def _ln2_call(x, y, g, b, tm=512):
    n, d = x.shape
    row = pl.BlockSpec((tm, d), lambda i: (i, 0))
    vec = pl.BlockSpec((1, d), lambda i: (0, 0))
    return pl.pallas_call(
        _ln2_kernel, grid=(n // tm,), in_specs=[row, row, vec, vec], out_specs=row,
        out_shape=jax.ShapeDtypeStruct((n, d), F32), compiler_params=_cparams("parallel"),
        name="ln_ffn")(x, y, g.reshape(1, d), b.reshape(1, d))


def _inproj_kernel(x_ref, w_ref, qkv_ref, gt_ref, hb_ref):
    xb = x_ref[...].astype(BF16)
    cw = 512
    for c in range(IN_WIDTH // cw):
        acc = _dot(xb, w_ref[:, c * cw:(c + 1) * cw])
        o = c * cw
        if o < 3 * NA_WIDTH:
            if o < NA_WIDTH:
                acc = acc * (NA_HEAD_DIM ** -0.5)
            qkv_ref[:, o:o + cw] = acc.astype(BF16)
        elif o < 3 * NA_WIDTH + GATE_WIDTH:
            o -= 3 * NA_WIDTH
            gt_ref[:, o:o + cw] = _sigmoid(acc).astype(BF16)
        else:
            o -= 3 * NA_WIDTH + GATE_WIDTH
            hb_ref[:, o:o + cw] = acc


def _inproj_call(x, w, tm=256):
    n, d = x.shape
    return pl.pallas_call(
        _inproj_kernel, grid=(n // tm,),
        in_specs=[pl.BlockSpec((tm, d), lambda i: (i, 0)),
                  pl.BlockSpec((d, IN_WIDTH), lambda i: (0, 0), pipeline_mode=pl.Buffered(1))],
        out_specs=[pl.BlockSpec((tm, 3 * NA_WIDTH), lambda i: (i, 0)),
                   pl.BlockSpec((tm, GATE_WIDTH), lambda i: (i, 0)),
                   pl.BlockSpec((tm, HB_WIDTH), lambda i: (i, 0))],
        out_shape=[jax.ShapeDtypeStruct((n, 3 * NA_WIDTH), BF16),
                   jax.ShapeDtypeStruct((n, GATE_WIDTH), BF16),
                   jax.ShapeDtypeStruct((n, HB_WIDTH), F32)],
        compiler_params=_cparams("parallel"), name="in_proj")(x, w)


def _na_kernel(q_ref, k_ref, v_ref, bias_ref, o_ref, s_ref):
    lane = lax.broadcasted_iota(jnp.int32, (1, 128), 1)
    lo = lane < NA_HEAD_DIM
    hi = jnp.logical_not(lo)
    for p in range(NA_HEADS // 2):
        cs = slice(p * 128, (p + 1) * 128)
        q2 = q_ref[:, cs]
        k2 = k_ref[:, cs]
        for hh in range(2):
            qm = jnp.where(hi if hh else lo, q2, jnp.zeros_like(q2))
            s = _dot_nt(qm, k2) + bias_ref[2 * p + hh]
            s_ref[2 * p + hh] = s - jnp.max(s, axis=-1, keepdims=True)
    for p in range(NA_HEADS // 2):
        cs = slice(p * 128, (p + 1) * 128)
        v2 = v_ref[:, cs]
        res = []
        for hh in range(2):
            e = jnp.exp(s_ref[2 * p + hh]).astype(BF16)
            va = jnp.where(hi if hh else lo, v2, jnp.ones_like(v2))
            res.append(_dot(e, va))
        o = jnp.where(lo, res[0], res[1])
        l = pltpu.roll(jnp.where(lo, res[1], res[0]), NA_HEAD_DIM, 1)
        o_ref[:, cs] = (o / l).astype(BF16)


def _na_call(qkv, bias, batch, rows):
    n = qkv.shape[0]
    assert rows >= NA_KH

    def start(r):
        return jnp.clip(r - NA_KH // 2, 0, rows - NA_KH)

    win = (pl.Element(NA_KH * GRID_W), pl.Element(NA_WIDTH))
    return pl.pallas_call(
        _na_kernel, grid=(batch, rows),
        in_specs=[pl.BlockSpec((GRID_W, NA_WIDTH), lambda b, r: (b * rows + r, 0)),
                  pl.BlockSpec(win, lambda b, r: ((b * rows + start(r)) * GRID_W, NA_WIDTH)),
                  pl.BlockSpec(win, lambda b, r: ((b * rows + start(r)) * GRID_W, 2 * NA_WIDTH)),
                  pl.BlockSpec((None, NA_HEADS, GRID_W, NA_KH * GRID_W),
                               lambda b, r: (r - start(r), 0, 0, 0))],
        out_specs=pl.BlockSpec((GRID_W, NA_WIDTH), lambda b, r: (b * rows + r, 0)),
        out_shape=jax.ShapeDtypeStruct((n, NA_WIDTH), BF16),
        scratch_shapes=[pltpu.VMEM((NA_HEADS, GRID_W, NA_KH * GRID_W), F32)],
        compiler_params=_cparams("parallel", "arbitrary"), name="na_attn")(qkv, qkv, qkv, bias)


def _na_bias_tables(rpb):
    c = np.arange(GRID_W)
    c_start = np.clip(c - NA_KW // 2, 0, GRID_W - NA_KW)
    col_mask = (c[None, :] >= c_start[:, None]) & (c[None, :] < c_start[:, None] + NA_KW)
    dc = np.clip(c[None, :] - c[:, None], -(NA_KW - 1), NA_KW - 1) + (NA_KW - 1)
    onehot = jnp.asarray(dc[:, :, None] == np.arange(2 * NA_KW - 1), F32)
    t = jnp.einsum("lhab,qkb->lhaqk", rpb.astype(F32), onehot, precision=lax.Precision.HIGHEST)
    t = jnp.where(col_mask[None, None, None], t, NEG_BIG)
    per_cls = [t[:, :, NA_KH - 1 - cls:2 * NA_KH - 1 - cls] for cls in range(NA_KH)]
    t = jnp.stack(per_cls, axis=1)
    t = t.transpose(0, 1, 2, 4, 3, 5)
    return t.reshape(rpb.shape[0], NA_KH, NA_HEADS, GRID_W, NA_KH * GRID_W)


def _hg_constants(reverse):
    n = HG_CHUNK
    t = np.arange(n)
    if reverse:
        tri = (t[None, :] >= t[:, None])
    else:
        tri = (t[None, :] <= t[:, None])
    masks = []
    for c in HG_LEVELS:
        blk = t // (2 * c)
        upper = (t % (2 * c)) >= c
        same = blk[:, None] == blk[None, :]
        if reverse:
            m = same & (~upper)[:, None] & upper[None, :]
        else:
            m = same & upper[:, None] & (~upper)[None, :]
        masks.append(m)
    masks.append(np.eye(n, dtype=bool))
    return jnp.asarray(tri, BF16), jnp.asarray(np.stack(masks), F32)


def _hgrn_kernel(*refs, reverse, final):
    if final:
        q_ref, z_ref, v_ref, lb_ref, tri_ref, lm_ref, gate_ref, prev_ref, ng_ref, o_ref, st_ref = refs
    else:
        q_ref, z_ref, v_ref, lb_ref, tri_ref, lm_ref, o_ref, st_ref = refs
    n = HG_CHUNK
    w = HG_WIDTH

    @pl.when(pl.program_id(1) == 0)
    def _():
        st_ref[...] = jnp.zeros_like(st_ref)

    z = z_ref[...]
    lb = lb_ref[...]
    f = lb + (1.0 - lb) * _sigmoid(z)
    g = jnp.log(jnp.maximum(f, F_MIN))
    kk = (1.0 - lb) * _sigmoid(-z)
    qv = q_ref[...]
    qq = qv * _sigmoid(qv)
    v_b = v_ref[...].astype(BF16)

    g1 = g.astype(BF16)
    r1 = g - g1.astype(F32)
    g2 = r1.astype(BF16)
    g3 = (r1 - g2.astype(F32)).astype(BF16)
    tri = tri_ref[...]
    b = _dot(tri, g1) + _dot(tri, g2) + _dot(tri, g3)
    last = 0 if reverse else n - 1
    b_last = b[last:last + 1, :]
    e_last = jnp.exp(b_last)

    row = lax.broadcasted_iota(jnp.int32, (n, 1), 0)
    qs, ks = [], []
    for c in HG_LEVELS:
        ref_row = c if reverse else c - 1
        if c == 1:
            d = jnp.where((row % 2) == (0 if reverse else 1), g, 0.0)
        elif c == 2:
            b3 = b.reshape(n // 8, 8, w)
            sub = lax.broadcasted_iota(jnp.int32, (1, 8, 1), 1)
            r_lo = jnp.broadcast_to(b3[:, ref_row:ref_row + 1, :], b3.shape)
            r_hi = jnp.broadcast_to(b3[:, ref_row + 4:ref_row + 5, :], b3.shape)
            d = -jnp.abs(b3 - jnp.where(sub < 4, r_lo, r_hi)).reshape(n, w)
        else:
            b3 = b.reshape(n // (2 * c), 2 * c, w)
            d = -jnp.abs(b3 - jnp.broadcast_to(b3[:, ref_row:ref_row + 1, :], b3.shape)).reshape(n, w)
        e = jnp.exp(d)
        qs.append((qq * e).astype(BF16))
        ks.append((kk * e).astype(BF16))
    qs.append(qq.astype(BF16))
    ks.append(kk.astype(BF16))
    q_in = (qq * jnp.exp(b)).astype(BF16)
    k_out = (kk * jnp.exp(b_last - b)).astype(BF16)

    outs = []
    for h in range(HG_HEADS):
        hs = slice(h * HG_HEAD_DIM, (h + 1) * HG_HEAD_DIM)
        a = None
        for li in range(len(qs)):
            term = lm_ref[li] * _dot_nt(qs[li][:, hs], ks[li][:, hs])
            a = term if a is None else a + term
        st = st_ref[h]
        o = _dot(a.astype(BF16), v_b[:, hs]) + _dot_nt(q_in[:, hs], st.astype(BF16))
        st_ref[h] = st * e_last[:, hs] + _dot_tn(v_b[:, hs], k_out[:, hs])
        outs.append(o)

    if not final:
        for h in range(HG_HEADS):
            o_ref[:, h * HG_HEAD_DIM:(h + 1) * HG_HEAD_DIM] = outs[h]
    else:
        gv = gate_ref[...]
        gs = gv * _sigmoid(gv)
        for h in range(HG_HEADS):
            hs = slice(h * HG_HEAD_DIM, (h + 1) * HG_HEAD_DIM)
            o = outs[h] + prev_ref[:, hs]
            o = o * lax.rsqrt(jnp.mean(o * o, axis=-1, keepdims=True) + RMS_EPS) * ng_ref[:, hs]
            o_ref[:, hs] = (o * gs[:, hs]).astype(BF16)


def _hgrn_call(hb, lb, z_col, batch, seq, reverse, prev=None, norm_g=None):
    n = hb.shape[0]
    nc = seq // HG_CHUNK
    final = prev is not None
    tri, masks = _hg_constants(reverse)

    def tok(b, c):
        return b * nc + ((nc - 1 - c) if reverse else c)

    def col(j):
        return pl.BlockSpec((HG_CHUNK, HG_WIDTH), lambda b, c: (tok(b, c), j))

    vec = pl.BlockSpec((1, HG_WIDTH), lambda b, c: (0, 0))
    in_specs = [col(0), col(z_col), col(3), vec,
                pl.BlockSpec((HG_CHUNK, HG_CHUNK), lambda b, c: (0, 0)),
                pl.BlockSpec(masks.shape, lambda b, c: (0, 0, 0))]
    args = [hb, hb, hb, lb.reshape(1, HG_WIDTH), tri, masks]
    if final:
        in_specs += [col(4), pl.BlockSpec((HG_CHUNK, HG_WIDTH), lambda b, c: (tok(b, c), 0)), vec]
        args += [hb, prev, norm_g.reshape(1, HG_WIDTH)]
    return pl.pallas_call(
        functools.partial(_hgrn_kernel, reverse=reverse, final=final),
        grid=(batch, nc), in_specs=in_specs,
        out_specs=pl.BlockSpec((HG_CHUNK, HG_WIDTH), lambda b, c: (tok(b, c), 0)),
        out_shape=jax.ShapeDtypeStruct((n, HG_WIDTH), BF16 if final else F32),
        scratch_shapes=[pltpu.VMEM((HG_HEADS, HG_HEAD_DIM, HG_HEAD_DIM), F32)],
        compiler_params=_cparams("arbitrary", "arbitrary"),
        name="hgrn_bwd" if reverse else "hgrn_fwd")(*args)


def _mix_kernel(na_ref, hg_ref, gt_ref, x_ref, wna_ref, whg_ref, wout_ref, g_ref, b_ref, wr_ref,
                x1_ref, x1b_ref, aff_ref):
    a = _dot(na_ref[...], wna_ref[...])
    r = _dot(hg_ref[...], whg_ref[...])
    mix = gt_ref[:, :D_MODEL].astype(F32) * a + gt_ref[:, D_MODEL:].astype(F32) * r
    y = _dot(mix.astype(BF16), wout_ref[...])
    x1 = _layer_norm(ALPHA * x_ref[...] + y, g_ref[...], b_ref[...])
    x1_ref[...] = x1
    x_hi = x1.astype(BF16)
    x1b_ref[...] = x_hi
    x_lo = (x1 - x_hi.astype(F32)).astype(BF16)
    p = _dot(x_hi, wr_ref[...]) + _dot(x_lo, wr_ref[...])
    logits = p[:, :N_EXPERTS] + p[:, N_EXPERTS:]
    e = jnp.exp(logits - jnp.max(logits, axis=-1, keepdims=True))
    aff_ref[...] = e / jnp.sum(e, axis=-1, keepdims=True)


def _mix_call(na, hg, gt, x, wna, whg, wout, g, b, wr2, tm=256):
    n, d = x.shape
    const = lambda shape: pl.BlockSpec(shape, lambda i: (0,) * len(shape))
    return pl.pallas_call(
        _mix_kernel, grid=(n // tm,),
        in_specs=[pl.BlockSpec((tm, NA_WIDTH), lambda i: (i, 0)),
                  pl.BlockSpec((tm, HG_WIDTH), lambda i: (i, 0)),
                  pl.BlockSpec((tm, 2 * D_MODEL), lambda i: (i, 0)),
                  pl.BlockSpec((tm, d), lambda i: (i, 0)),
                  const((NA_WIDTH, d)), const((HG_WIDTH, d)), const((d, d)),
                  const((1, d)), const((1, d)), const((d, 2 * N_EXPERTS))],
        out_specs=[pl.BlockSpec((tm, d), lambda i: (i, 0)),
                   pl.BlockSpec((tm, d), lambda i: (i, 0)),
                   pl.BlockSpec((tm, N_EXPERTS), lambda i: (i, 0))],
        out_shape=[jax.ShapeDtypeStruct((n, d), F32), jax.ShapeDtypeStruct((n, d), BF16),
                   jax.ShapeDtypeStruct((n, N_EXPERTS), F32)],
        compiler_params=_cparams("parallel"), name="mix_router")(
            na, hg, gt, x, wna, whg, wout, g.reshape(1, d), b.reshape(1, d), wr2)


def _ffn_kernel(xs_ref, wg_ref, wu_ref, wd_ref, gate_ref, o_ref):
    k = pl.program_id(2)
    xs = xs_ref[...]
    hg = _dot(xs, wg_ref[...])
    hu = _dot(xs, wu_ref[...])
    part = _dot((hg * _sigmoid(hg) * hu).astype(BF16), wd_ref[...])

    @pl.when(k == 0)
    def _():
        o_ref[...] = part

    @pl.when(k > 0)
    def _():
        o_ref[...] += part

    @pl.when(k == pl.num_programs(2) - 1)
    def _():
        o_ref[...] = o_ref[...] * gate_ref[...]


def _ffn_call(xs, wg, wu, wd, gate, tc=1024, tf=512):
    e, c, d = xs.shape
    f = wg.shape[-1]
    tc = min(tc, c)
    return pl.pallas_call(
        _ffn_kernel, grid=(e, c // tc, f // tf),
        in_specs=[pl.BlockSpec((None, tc, d), lambda i, j, k: (i, j, 0)),
                  pl.BlockSpec((None, d, tf), lambda i, j, k: (i, 0, k)),
                  pl.BlockSpec((None, d, tf), lambda i, j, k: (i, 0, k)),
                  pl.BlockSpec((None, tf, d), lambda i, j, k: (i, k, 0)),
                  pl.BlockSpec((None, tc, 1), lambda i, j, k: (i, j, 0))],
        out_specs=pl.BlockSpec((None, tc, d), lambda i, j, k: (i, j, 0)),
        out_shape=jax.ShapeDtypeStruct((e, c, d), F32),
        compiler_params=_cparams("parallel", "parallel", "arbitrary"), name="expert_ffn")(
            xs, wg, wu, wd, gate.reshape(e, c, 1))


def _hi_lo(w):
    hi = w.astype(BF16)
    lo = (w - hi.astype(F32)).astype(BF16)
    return jnp.concatenate([hi, lo], axis=-1)


def _lower_bounds(p):
    sm = jax.nn.softmax(p.astype(F32), axis=0)
    return jnp.cumsum(sm, axis=0) - sm[0:1]


def _trunk(x, prm):
    batch, seq, d = x.shape
    n = batch * seq
    rows = seq // GRID_W
    cap = EC_FACTOR * n // N_EXPERTS
    x = _ln_call(x.reshape(n, d), prm["ln_in_g"], prm["ln_in_b"])
    for l in range(DEPTH):
        qkv, gt, hb = _inproj_call(x, prm["w_in"][l])
        na = _na_call(qkv, prm["na_bias"][l], batch, rows)
        o_f = _hgrn_call(hb, prm["lb_fwd"][l], 1, batch, seq, reverse=False)
        hg = _hgrn_call(hb, prm["lb_bwd"][l], 2, batch, seq, reverse=True, prev=o_f,
                        norm_g=prm["hg_norm_g"][l])
        x1, x1b, aff = _mix_call(na, hg, gt, x, prm["w_branch_na"][l], prm["w_branch_hg"][l],
                                 prm["w_out"][l], prm["ln1_g"][l], prm["ln1_b"][l], prm["w_router2"][l])
        gate, idx = lax.top_k(aff.T, cap)
        xs = x1b[idx]
        out = _ffn_call(xs, prm["w_gate_e"][l], prm["w_up_e"][l], prm["w_down_e"][l], gate)
        y = jnp.zeros((n, d), F32).at[idx.reshape(-1)].add(out.reshape(-1, d))
        x = _ln2_call(x1, y, prm["ln2_g"][l], prm["ln2_b"][l])
    return x.reshape(batch, seq, d)


@jax.jit
def kernel(x_prompt, x_sample, ln_in_g, ln_in_b, w_in, na_rpb, hg_lb_fwd, hg_lb_bwd, hg_norm_g,
           w_branch_na, w_branch_hg, w_out, ln1_g, ln1_b, w_router, w_gate_e, w_up_e, w_down_e,
           ln2_g, ln2_b):
    na_w, hg_w = 3 * NA_WIDTH, 5 * HG_WIDTH
    w_in_r = jnp.concatenate([w_in[..., :na_w], w_in[..., na_w + hg_w:], w_in[..., na_w:na_w + hg_w]],
                             axis=-1).astype(BF16)
    prm = dict(
        ln_in_g=ln_in_g, ln_in_b=ln_in_b, w_in=w_in_r, na_bias=_na_bias_tables(na_rpb),
        lb_fwd=_lower_bounds(hg_lb_fwd), lb_bwd=_lower_bounds(hg_lb_bwd), hg_norm_g=hg_norm_g,
        w_branch_na=w_branch_na.astype(BF16), w_branch_hg=w_branch_hg.astype(BF16),
        w_out=w_out.astype(BF16), ln1_g=ln1_g, ln1_b=ln1_b,
        w_router2=_hi_lo(w_router),
        w_gate_e=w_gate_e.astype(BF16), w_up_e=w_up_e.astype(BF16), w_down_e=w_down_e.astype(BF16),
        ln2_g=ln2_g, ln2_b=ln2_b)
    return _trunk(x_prompt, prm), _trunk(x_sample, prm)
```

```python
import functools

import jax
import jax.numpy as jnp
import numpy as np
from jax import lax
from jax.experimental import pallas as pl
from jax.experimental.pallas import tpu as pltpu

F32 = jnp.float32
BF16 = jnp.bfloat16

D_MODEL = 1024
DEPTH = 4
GRID_W = 64
NA_HEADS = 8
NA_HEAD_DIM = 64
NA_WIDTH = NA_HEADS * NA_HEAD_DIM
NA_KH = 8
NA_KW = 16
HG_HEADS = 4
HG_HEAD_DIM = 128
HG_WIDTH = HG_HEADS * HG_HEAD_DIM
HG_CHUNK = 64
N_EXPERTS = 16
EC_FACTOR = 2
D_FF = 2 * D_MODEL
IN_WIDTH = 3 * NA_WIDTH + 5 * HG_WIDTH + 2 * D_MODEL
ALPHA = (2 * DEPTH) ** 0.25
LN_EPS = 1e-5
RMS_EPS = 1e-6
NEG_BIG = -1e30
F_MIN = 1e-30

GATE_WIDTH = 2 * D_MODEL
HB_WIDTH = 5 * HG_WIDTH
HG_LEVELS = (32, 16, 8, 4, 2, 1)
VMEM_LIMIT = 48 * 1024 * 1024
RANK_TILE = 256
SLOTS_PER_PASS = 64
WIN = SLOTS_PER_PASS + 16


def _cparams(*sem):
    return pltpu.CompilerParams(dimension_semantics=sem, vmem_limit_bytes=VMEM_LIMIT)


def _sigmoid(x):
    return 1.0 / (1.0 + jnp.exp(-x))


def _layer_norm(x, g, b):
    mu = jnp.mean(x, axis=-1, keepdims=True)
    xc = x - mu
    var = jnp.mean(xc * xc, axis=-1, keepdims=True)
    return xc * lax.rsqrt(var + LN_EPS) * g + b


def _dot(a, b):
    return jnp.dot(a, b, preferred_element_type=F32)


def _dot_nt(a, b):
    return lax.dot_general(a, b, (((1,), (1,)), ((), ())), preferred_element_type=F32)


def _dot_tn(a, b):
    return lax.dot_general(a, b, (((0,), (0,)), ((), ())), preferred_element_type=F32)


def _ln_kernel(x_ref, g_ref, b_ref, o_ref):
    o_ref[...] = _layer_norm(x_ref[...], g_ref[...], b_ref[...])


def _ln_call(x, g, b, tm=512):
    n, d = x.shape
    row = pl.BlockSpec((tm, d), lambda i: (i, 0))
    vec = pl.BlockSpec((1, d), lambda i: (0, 0))
    return pl.pallas_call(
        _ln_kernel, grid=(n // tm,), in_specs=[row, vec, vec], out_specs=row,
        out_shape=jax.ShapeDtypeStruct((n, d), F32), compiler_params=_cparams("parallel"),
        name="ln_in")(x, g.reshape(1, d), b.reshape(1, d))


def _inproj_kernel(x_ref, w_ref, qkv_ref, gt_ref, hb_ref):
    xb = x_ref[...].astype(BF16)
    cw = 512
    for c in range(IN_WIDTH // cw):
        acc = _dot(xb, w_ref[:, c * cw:(c + 1) * cw])
        o = c * cw
        if o < 3 * NA_WIDTH:
            if o < NA_WIDTH:
                acc = acc * (NA_HEAD_DIM ** -0.5)
            qkv_ref[:, o:o + cw] = acc.astype(BF16)
        elif o < 3 * NA_WIDTH + GATE_WIDTH:
            o -= 3 * NA_WIDTH
            gt_ref[:, o:o + cw] = _sigmoid(acc).astype(BF16)
        else:
            o -= 3 * NA_WIDTH + GATE_WIDTH
            hb_ref[:, o:o + cw] = acc


def _inproj_call(x, w, tm=256):
    n, d = x.shape
    return pl.pallas_call(
        _inproj_kernel, grid=(n // tm,),
        in_specs=[pl.BlockSpec((tm, d), lambda i: (i, 0)),
                  pl.BlockSpec((d, IN_WIDTH), lambda i: (0, 0), pipeline_mode=pl.Buffered(1))],
        out_specs=[pl.BlockSpec((tm, 3 * NA_WIDTH), lambda i: (i, 0)),
                   pl.BlockSpec((tm, GATE_WIDTH), lambda i: (i, 0)),
                   pl.BlockSpec((tm, HB_WIDTH), lambda i: (i, 0))],
        out_shape=[jax.ShapeDtypeStruct((n, 3 * NA_WIDTH), BF16),
                   jax.ShapeDtypeStruct((n, GATE_WIDTH), BF16),
                   jax.ShapeDtypeStruct((n, HB_WIDTH), F32)],
        compiler_params=_cparams("parallel"), name="in_proj")(x, w)


def _na_kernel(q_ref, k_ref, v_ref, bias_ref, o_ref, s_ref):
    lane = lax.broadcasted_iota(jnp.int32, (1, 128), 1)
    lo = lane < NA_HEAD_DIM
    hi = jnp.logical_not(lo)
    for p in range(NA_HEADS // 2):
        cs = slice(p * 128, (p + 1) * 128)
        q2 = q_ref[:, cs]
        k2 = k_ref[:, cs]
        for hh in range(2):
            qm = jnp.where(hi if hh else lo, q2, jnp.zeros_like(q2))
            s = _dot_nt(qm, k2) + bias_ref[2 * p + hh]
            s_ref[2 * p + hh] = s - jnp.max(s, axis=-1, keepdims=True)
    for p in range(NA_HEADS // 2):
        cs = slice(p * 128, (p + 1) * 128)
        v2 = v_ref[:, cs]
        res = []
        for hh in range(2):
            e = jnp.exp(s_ref[2 * p + hh]).astype(BF16)
            va = jnp.where(hi if hh else lo, v2, jnp.ones_like(v2))
            res.append(_dot(e, va))
        o = jnp.where(lo, res[0], res[1])
        l = pltpu.roll(jnp.where(lo, res[1], res[0]), NA_HEAD_DIM, 1)
        o_ref[:, cs] = (o / l).astype(BF16)


def _na_call(qkv, bias, batch, rows):
    n = qkv.shape[0]
    assert rows >= NA_KH

    def start(r):
        return jnp.clip(r - NA_KH // 2, 0, rows - NA_KH)

    win = (pl.Element(NA_KH * GRID_W), pl.Element(NA_WIDTH))
    return pl.pallas_call(
        _na_kernel, grid=(batch, rows),
        in_specs=[pl.BlockSpec((GRID_W, NA_WIDTH), lambda b, r: (b * rows + r, 0)),
                  pl.BlockSpec(win, lambda b, r: ((b * rows + start(r)) * GRID_W, NA_WIDTH)),
                  pl.BlockSpec(win, lambda b, r: ((b * rows + start(r)) * GRID_W, 2 * NA_WIDTH)),
                  pl.BlockSpec((None, NA_HEADS, GRID_W, NA_KH * GRID_W),
                               lambda b, r: (r - start(r), 0, 0, 0))],
        out_specs=pl.BlockSpec((GRID_W, NA_WIDTH), lambda b, r: (b * rows + r, 0)),
        out_shape=jax.ShapeDtypeStruct((n, NA_WIDTH), BF16),
        scratch_shapes=[pltpu.VMEM((NA_HEADS, GRID_W, NA_KH * GRID_W), F32)],
        compiler_params=_cparams("parallel", "arbitrary"), name="na_attn")(qkv, qkv, qkv, bias)


def _na_bias_tables(rpb):
    c = np.arange(GRID_W)
    c_start = np.clip(c - NA_KW // 2, 0, GRID_W - NA_KW)
    col_mask = (c[None, :] >= c_start[:, None]) & (c[None, :] < c_start[:, None] + NA_KW)
    dc = np.clip(c[None, :] - c[:, None], -(NA_KW - 1), NA_KW - 1) + (NA_KW - 1)
    onehot = jnp.asarray(dc[:, :, None] == np.arange(2 * NA_KW - 1), F32)
    t = jnp.einsum("lhab,qkb->lhaqk", rpb.astype(F32), onehot, precision=lax.Precision.HIGHEST)
    t = jnp.where(col_mask[None, None, None], t, NEG_BIG)
    per_cls = [t[:, :, NA_KH - 1 - cls:2 * NA_KH - 1 - cls] for cls in range(NA_KH)]
    t = jnp.stack(per_cls, axis=1)
    t = t.transpose(0, 1, 2, 4, 3, 5)
    return t.reshape(rpb.shape[0], NA_KH, NA_HEADS, GRID_W, NA_KH * GRID_W)


def _hg_constants(reverse):
    n = HG_CHUNK
    t = np.arange(n)
    if reverse:
        tri = (t[None, :] >= t[:, None])
    else:
        tri = (t[None, :] <= t[:, None])
    masks = []
    for c in HG_LEVELS:
        blk = t // (2 * c)
        upper = (t % (2 * c)) >= c
        same = blk[:, None] == blk[None, :]
        if reverse:
            m = same & (~upper)[:, None] & upper[None, :]
        else:
            m = same & upper[:, None] & (~upper)[None, :]
        masks.append(m)
    masks.append(np.eye(n, dtype=bool))
    return jnp.asarray(tri, BF16), jnp.asarray(np.stack(masks), F32)


def _hgrn_kernel(*refs, reverse, final):
    if final:
        q_ref, z_ref, v_ref, lb_ref, tri_ref, lm_ref, gate_ref, prev_ref, ng_ref, o_ref, st_ref = refs
    else:
        q_ref, z_ref, v_ref, lb_ref, tri_ref, lm_ref, o_ref, st_ref = refs
    n = HG_CHUNK
    w = HG_WIDTH

    @pl.when(pl.program_id(1) == 0)
    def _():
        st_ref[...] = jnp.zeros_like(st_ref)

    z = z_ref[...]
    lb = lb_ref[...]
    f = lb + (1.0 - lb) * _sigmoid(z)
    g = jnp.log(jnp.maximum(f, F_MIN))
    kk = (1.0 - lb) * _sigmoid(-z)
    qv = q_ref[...]
    qq = qv * _sigmoid(qv)
    v_b = v_ref[...].astype(BF16)

    g1 = g.astype(BF16)
    r1 = g - g1.astype(F32)
    g2 = r1.astype(BF16)
    g3 = (r1 - g2.astype(F32)).astype(BF16)
    tri = tri_ref[...]
    b = _dot(tri, g1) + _dot(tri, g2) + _dot(tri, g3)
    last = 0 if reverse else n - 1
    b_last = b[last:last + 1, :]
    e_last = jnp.exp(b_last)

    row = lax.broadcasted_iota(jnp.int32, (n, 1), 0)
    qs, ks = [], []
    for c in HG_LEVELS:
        ref_row = c if reverse else c - 1
        if c == 1:
            d = jnp.where((row % 2) == (0 if reverse else 1), g, 0.0)
        elif c == 2:
            b3 = b.reshape(n // 8, 8, w)
            sub = lax.broadcasted_iota(jnp.int32, (1, 8, 1), 1)
            r_lo = jnp.broadcast_to(b3[:, ref_row:ref_row + 1, :], b3.shape)
            r_hi = jnp.broadcast_to(b3[:, ref_row + 4:ref_row + 5, :], b3.shape)
            d = -jnp.abs(b3 - jnp.where(sub < 4, r_lo, r_hi)).reshape(n, w)
        else:
            b3 = b.reshape(n // (2 * c), 2 * c, w)
            d = -jnp.abs(b3 - jnp.broadcast_to(b3[:, ref_row:ref_row + 1, :], b3.shape)).reshape(n, w)
        e = jnp.exp(d)
        qs.append((qq * e).astype(BF16))
        ks.append((kk * e).astype(BF16))
    qs.append(qq.astype(BF16))
    ks.append(kk.astype(BF16))
    q_in = (qq * jnp.exp(b)).astype(BF16)
    k_out = (kk * jnp.exp(b_last - b)).astype(BF16)

    outs = []
    for h in range(HG_HEADS):
        hs = slice(h * HG_HEAD_DIM, (h + 1) * HG_HEAD_DIM)
        a = None
        for li in range(len(qs)):
            term = lm_ref[li] * _dot_nt(qs[li][:, hs], ks[li][:, hs])
            a = term if a is None else a + term
        st = st_ref[h]
        o = _dot(a.astype(BF16), v_b[:, hs]) + _dot_nt(q_in[:, hs], st.astype(BF16))
        st_ref[h] = st * e_last[:, hs] + _dot_tn(v_b[:, hs], k_out[:, hs])
        outs.append(o)

    if not final:
        for h in range(HG_HEADS):
            o_ref[:, h * HG_HEAD_DIM:(h + 1) * HG_HEAD_DIM] = outs[h]
    else:
        gv = gate_ref[...]
        gs = gv * _sigmoid(gv)
        for h in range(HG_HEADS):
            hs = slice(h * HG_HEAD_DIM, (h + 1) * HG_HEAD_DIM)
            o = outs[h] + prev_ref[:, hs]
            o = o * lax.rsqrt(jnp.mean(o * o, axis=-1, keepdims=True) + RMS_EPS) * ng_ref[:, hs]
            o_ref[:, hs] = (o * gs[:, hs]).astype(BF16)


def _hgrn_call(hb, lb, z_col, batch, seq, reverse, prev=None, norm_g=None):
    n = hb.shape[0]
    nc = seq // HG_CHUNK
    final = prev is not None
    tri, masks = _hg_constants(reverse)

    def tok(b, c):
        return b * nc + ((nc - 1 - c) if reverse else c)

    def col(j):
        return pl.BlockSpec((HG_CHUNK, HG_WIDTH), lambda b, c: (tok(b, c), j))

    vec = pl.BlockSpec((1, HG_WIDTH), lambda b, c: (0, 0))
    in_specs = [col(0), col(z_col), col(3), vec,
                pl.BlockSpec((HG_CHUNK, HG_CHUNK), lambda b, c: (0, 0)),
                pl.BlockSpec(masks.shape, lambda b, c: (0, 0, 0))]
    args = [hb, hb, hb, lb.reshape(1, HG_WIDTH), tri, masks]
    if final:
        in_specs += [col(4), pl.BlockSpec((HG_CHUNK, HG_WIDTH), lambda b, c: (tok(b, c), 0)), vec]
        args += [hb, prev, norm_g.reshape(1, HG_WIDTH)]
    return pl.pallas_call(
        functools.partial(_hgrn_kernel, reverse=reverse, final=final),
        grid=(batch, nc), in_specs=in_specs,
        out_specs=pl.BlockSpec((HG_CHUNK, HG_WIDTH), lambda b, c: (tok(b, c), 0)),
        out_shape=jax.ShapeDtypeStruct((n, HG_WIDTH), BF16 if final else F32),
        scratch_shapes=[pltpu.VMEM((HG_HEADS, HG_HEAD_DIM, HG_HEAD_DIM), F32)],
        compiler_params=_cparams("arbitrary", "arbitrary"),
        name="hgrn_bwd" if reverse else "hgrn_fwd")(*args)


def _mix_kernel(na_ref, hg_ref, gt_ref, x_ref, wna_ref, whg_ref, wout_ref, g_ref, b_ref, wr_ref,
                x1_ref, x1b_ref, aff_ref):
    a = _dot(na_ref[...], wna_ref[...])
    r = _dot(hg_ref[...], whg_ref[...])
    mix = gt_ref[:, :D_MODEL].astype(F32) * a + gt_ref[:, D_MODEL:].astype(F32) * r
    y = _dot(mix.astype(BF16), wout_ref[...])
    x1 = _layer_norm(ALPHA * x_ref[...] + y, g_ref[...], b_ref[...])
    x1_ref[...] = x1
    x_hi = x1.astype(BF16)
    x1b_ref[...] = x_hi
    x_lo = (x1 - x_hi.astype(F32)).astype(BF16)
    p = _dot(x_hi, wr_ref[...]) + _dot(x_lo, wr_ref[...])
    logits = p[:, :N_EXPERTS] + p[:, N_EXPERTS:]
    e = jnp.exp(logits - jnp.max(logits, axis=-1, keepdims=True))
    aff_ref[...] = e / jnp.sum(e, axis=-1, keepdims=True)


def _mix_call(na, hg, gt, x, wna, whg, wout, g, b, wr2, tm=256):
    n, d = x.shape
    const = lambda shape: pl.BlockSpec(shape, lambda i: (0,) * len(shape))
    return pl.pallas_call(
        _mix_kernel, grid=(n // tm,),
        in_specs=[pl.BlockSpec((tm, NA_WIDTH), lambda i: (i, 0)),
                  pl.BlockSpec((tm, HG_WIDTH), lambda i: (i, 0)),
                  pl.BlockSpec((tm, 2 * D_MODEL), lambda i: (i, 0)),
                  pl.BlockSpec((tm, d), lambda i: (i, 0)),
                  const((NA_WIDTH, d)), const((HG_WIDTH, d)), const((d, d)),
                  const((1, d)), const((1, d)), const((d, 2 * N_EXPERTS))],
        out_specs=[pl.BlockSpec((tm, d), lambda i: (i, 0)),
                   pl.BlockSpec((tm, d), lambda i: (i, 0)),
                   pl.BlockSpec((tm, N_EXPERTS), lambda i: (i, 0))],
        out_shape=[jax.ShapeDtypeStruct((n, d), F32), jax.ShapeDtypeStruct((n, d), BF16),
                   jax.ShapeDtypeStruct((n, N_EXPERTS), F32)],
        compiler_params=_cparams("parallel"), name="mix_router")(
            na, hg, gt, x, wna, whg, wout, g.reshape(1, d), b.reshape(1, d), wr2)


def _ffn_kernel(xs_ref, wg_ref, wu_ref, wd_ref, gate_ref, o_ref, acc_ref):
    k = pl.program_id(2)
    xs = xs_ref[...]
    hg = _dot(xs, wg_ref[...])
    hu = _dot(xs, wu_ref[...])
    part = _dot((hg * _sigmoid(hg) * hu).astype(BF16), wd_ref[...])

    @pl.when(k == 0)
    def _():
        acc_ref[...] = part

    @pl.when(k > 0)
    def _():
        acc_ref[...] += part

    @pl.when(k == pl.num_programs(2) - 1)
    def _():
        o_ref[...] = (acc_ref[...] * gate_ref[...]).astype(BF16)


def _ffn_call(xs, wg, wu, wd, gate, tc=1024, tf=512):
    e, c, d = xs.shape
    f = wg.shape[-1]
    tc = min(tc, c)
    return pl.pallas_call(
        _ffn_kernel, grid=(e, c // tc, f // tf),
        in_specs=[pl.BlockSpec((None, tc, d), lambda i, j, k: (i, j, 0)),
                  pl.BlockSpec((None, d, tf), lambda i, j, k: (i, 0, k)),
                  pl.BlockSpec((None, d, tf), lambda i, j, k: (i, 0, k)),
                  pl.BlockSpec((None, tf, d), lambda i, j, k: (i, k, 0)),
                  pl.BlockSpec((None, tc, 1), lambda i, j, k: (i, j, 0))],
        out_specs=pl.BlockSpec((None, tc, d), lambda i, j, k: (i, j, 0)),
        out_shape=jax.ShapeDtypeStruct((e, c, d), BF16),
        scratch_shapes=[pltpu.VMEM((tc, d), F32)],
        compiler_params=_cparams("parallel", "parallel", "arbitrary"), name="expert_ffn")(
            xs, wg, wu, wd, gate.reshape(e, c, 1))


def _rank_kernel(aff_ref, tau_ref, need_ref, lrank_ref, base_ref, ceq_ref, csel_ref):
    @pl.when(pl.program_id(0) == 0)
    def _():
        ceq_ref[...] = jnp.zeros_like(ceq_ref)
        csel_ref[...] = jnp.zeros_like(csel_ref)

    a = aff_ref[...]
    tau = tau_ref[:, 0:1]
    u = lax.broadcasted_iota(jnp.int32, (RANK_TILE, RANK_TILE), 0)
    t = lax.broadcasted_iota(jnp.int32, (RANK_TILE, RANK_TILE), 1)
    before = jnp.where(u < t, 1.0, 0.0).astype(BF16)
    ones = jnp.ones((RANK_TILE, 128), BF16)
    eq = a == tau
    eq_f = jnp.where(eq, 1.0, 0.0).astype(BF16)
    eq_before = ceq_ref[:, 0:1] + _dot(eq_f, before)
    sel = jnp.logical_or(a > tau, jnp.logical_and(eq, eq_before < need_ref[:, 0:1]))
    sel_f = jnp.where(sel, 1.0, 0.0).astype(BF16)
    lrank_ref[...] = jnp.where(sel, _dot(sel_f, before), -1.0).astype(jnp.int32)
    base_ref[...] = csel_ref[...].astype(jnp.int32)
    ceq_ref[...] += _dot(eq_f, ones)
    csel_ref[...] += _dot(sel_f, ones)


def _rank_call(aff_t, tau, need):
    e, n = aff_t.shape
    nt = n // RANK_TILE
    rep = lambda v: jnp.broadcast_to(v.astype(F32).reshape(e, 1), (e, 128))
    vec = pl.BlockSpec((e, 128), lambda j: (0, 0))
    return pl.pallas_call(
        _rank_kernel, grid=(nt,),
        in_specs=[pl.BlockSpec((e, RANK_TILE), lambda j: (0, j)), vec, vec],
        out_specs=[pl.BlockSpec((e, RANK_TILE), lambda j: (0, j)),
                   pl.BlockSpec((None, e, 128), lambda j: (j, 0, 0))],
        out_shape=[jax.ShapeDtypeStruct((e, n), jnp.int32),
                   jax.ShapeDtypeStruct((nt, e, 128), jnp.int32)],
        scratch_shapes=[pltpu.VMEM((e, 128), F32), pltpu.VMEM((e, 128), F32)],
        compiler_params=_cparams("arbitrary"), name="slot_rank")(aff_t, rep(tau), rep(need))


def _combine_kernel(base_ref, npass_ref, x_ref, lrank_ref, g_ref, b_ref, out_hbm, o_ref,
                    buf_ref, acc_ref, sem_ref, *, cap):
    j = pl.program_id(0)
    nt = pl.num_programs(0)

    def win_start(tile, e, p):
        r = base_ref[e * nt + tile] + p * SLOTS_PER_PASS
        return jnp.minimum((r // 16) * 16, cap - WIN)

    def window_copy(tile, e, p, slot):
        a = pl.multiple_of(win_start(tile, e, p), 16)
        return pltpu.make_async_copy(out_hbm.at[e, pl.ds(a, WIN), :],
                                     buf_ref.at[slot, pl.ds(e * WIN, WIN), :], sem_ref.at[slot])

    def start_all(tile, p, slot):
        for e in range(N_EXPERTS):
            window_copy(tile, e, p, slot).start()

    def wait_all(tile, p, slot):
        for e in range(N_EXPERTS):
            window_copy(tile, e, p, slot).wait()

    def add_pass(p, slot):
        row = lax.broadcasted_iota(jnp.int32, (WIN, RANK_TILE), 0)
        sel = []
        for e in range(N_EXPERTS):
            lr = lrank_ref[e:e + 1, :]
            off = base_ref[e * nt + j] - win_start(j, e, p)
            pos = jnp.where(jnp.logical_and(lr >= p * SLOTS_PER_PASS, lr < (p + 1) * SLOTS_PER_PASS),
                            lr + off, -1)
            sel.append(jnp.where(row == pos, 1.0, 0.0).astype(BF16))
        acc_ref[...] += _dot_tn(jnp.concatenate(sel, axis=0), buf_ref[slot])

    slot = j % 2

    @pl.when(j == 0)
    def _():
        start_all(0, 0, 0)

    @pl.when(j + 1 < nt)
    def _():
        start_all(j + 1, 0, 1 - slot)

    acc_ref[...] = jnp.zeros_like(acc_ref)
    wait_all(j, 0, slot)
    add_pass(0, slot)

    def extra(p, carry):
        start_all(j, p, slot)
        wait_all(j, p, slot)
        add_pass(p, slot)
        return carry

    lax.fori_loop(1, npass_ref[j], extra, 0)
    o_ref[...] = _layer_norm(ALPHA * x_ref[...] + acc_ref[...], g_ref[...], b_ref[...])


def _combine_call(x1, lrank, base, out, g, b):
    n, d = x1.shape
    e, cap, _ = out.shape
    nt = n // RANK_TILE
    assert cap >= WIN and cap % 16 == 0
    cnt = jnp.concatenate([base[:, 1:], jnp.full((e, 1), cap, jnp.int32)], axis=1) - base
    npass = jnp.maximum(1, (jnp.max(cnt, axis=0) + SLOTS_PER_PASS - 1) // SLOTS_PER_PASS).astype(jnp.int32)
    vec = pl.BlockSpec((1, d), lambda j, *_: (0, 0))
    grid_spec = pltpu.PrefetchScalarGridSpec(
        num_scalar_prefetch=2, grid=(nt,),
        in_specs=[pl.BlockSpec((RANK_TILE, d), lambda j, *_: (j, 0)),
                  pl.BlockSpec((e, RANK_TILE), lambda j, *_: (0, j)),
                  vec, vec, pl.BlockSpec(memory_space=pl.ANY)],
        out_specs=pl.BlockSpec((RANK_TILE, d), lambda j, *_: (j, 0)),
        scratch_shapes=[pltpu.VMEM((2, e * WIN, d), BF16), pltpu.VMEM((RANK_TILE, d), F32),
                        pltpu.SemaphoreType.DMA((2,))])
    return pl.pallas_call(
        functools.partial(_combine_kernel, cap=cap), grid_spec=grid_spec,
        out_shape=jax.ShapeDtypeStruct((n, d), F32),
        compiler_params=_cparams("arbitrary"), name="combine_ln")(
            base.reshape(-1), npass, x1, lrank, g.reshape(1, d), b.reshape(1, d), out)


def _hi_lo(w):
    hi = w.astype(BF16)
    lo = (w - hi.astype(F32)).astype(BF16)
    return jnp.concatenate([hi, lo], axis=-1)


def _lower_bounds(p):
    sm = jax.nn.softmax(p.astype(F32), axis=0)
    return jnp.cumsum(sm, axis=0) - sm[0:1]


def _trunk(x, prm):
    batch, seq, d = x.shape
    n = batch * seq
    rows = seq // GRID_W
    cap = EC_FACTOR * n // N_EXPERTS
    x = _ln_call(x.reshape(n, d), prm["ln_in_g"], prm["ln_in_b"])
    for l in range(DEPTH):
        qkv, gt, hb = _inproj_call(x, prm["w_in"][l])
        na = _na_call(qkv, prm["na_bias"][l], batch, rows)
        o_f = _hgrn_call(hb, prm["lb_fwd"][l], 1, batch, seq, reverse=False)
        hg = _hgrn_call(hb, prm["lb_bwd"][l], 2, batch, seq, reverse=True, prev=o_f,
                        norm_g=prm["hg_norm_g"][l])
        x1, x1b, aff = _mix_call(na, hg, gt, x, prm["w_branch_na"][l], prm["w_branch_hg"][l],
                                 prm["w_out"][l], prm["ln1_g"][l], prm["ln1_b"][l], prm["w_router2"][l])
        aff_t = aff.T
        gate, idx = lax.top_k(aff_t, cap)
        idx, gate = lax.sort((idx, gate), dimension=1, num_keys=1)
        tau = gate.min(axis=1)
        need = cap - jnp.sum(aff_t > tau[:, None], axis=1)
        lrank, base = _rank_call(aff_t, tau, need)
        xs = x1b[idx]
        out = _ffn_call(xs, prm["w_gate_e"][l], prm["w_up_e"][l], prm["w_down_e"][l], gate)
        x = _combine_call(x1, lrank, base[:, :, 0].T, out, prm["ln2_g"][l], prm["ln2_b"][l])
    return x.reshape(batch, seq, d)


@jax.jit
def kernel(x_prompt, x_sample, ln_in_g, ln_in_b, w_in, na_rpb, hg_lb_fwd, hg_lb_bwd, hg_norm_g,
           w_branch_na, w_branch_hg, w_out, ln1_g, ln1_b, w_router, w_gate_e, w_up_e, w_down_e,
           ln2_g, ln2_b):
    na_w, hg_w = 3 * NA_WIDTH, 5 * HG_WIDTH
    w_in_r = jnp.concatenate([w_in[..., :na_w], w_in[..., na_w + hg_w:], w_in[..., na_w:na_w + hg_w]],
                             axis=-1).astype(BF16)
    prm = dict(
        ln_in_g=ln_in_g, ln_in_b=ln_in_b, w_in=w_in_r, na_bias=_na_bias_tables(na_rpb),
        lb_fwd=_lower_bounds(hg_lb_fwd), lb_bwd=_lower_bounds(hg_lb_bwd), hg_norm_g=hg_norm_g,
        w_branch_na=w_branch_na.astype(BF16), w_branch_hg=w_branch_hg.astype(BF16),
        w_out=w_out.astype(BF16), ln1_g=ln1_g, ln1_b=ln1_b,
        w_router2=_hi_lo(w_router),
        w_gate_e=w_gate_e.astype(BF16), w_up_e=w_up_e.astype(BF16), w_down_e=w_down_e.astype(BF16),
        ln2_g=ln2_g, ln2_b=ln2_b)
    return _trunk(x_prompt, prm), _trunk(x_sample, prm)
```

```python
import functools

import jax
import jax.numpy as jnp
import numpy as np
from jax import lax
from jax.experimental import pallas as pl
from jax.experimental.pallas import tpu as pltpu

F32 = jnp.float32
BF16 = jnp.bfloat16

D_MODEL = 1024
DEPTH = 4
GRID_W = 64
NA_HEADS = 8
NA_HEAD_DIM = 64
NA_WIDTH = NA_HEADS * NA_HEAD_DIM
NA_KH = 8
NA_KW = 16
NA_ROWS = 4
NA_WIN = NA_KH + NA_ROWS - 1
HG_HEADS = 4
HG_HEAD_DIM = 128
HG_WIDTH = HG_HEADS * HG_HEAD_DIM
HG_CHUNK = 64
N_EXPERTS = 16
EC_FACTOR = 2
D_FF = 2 * D_MODEL
IN_WIDTH = 3 * NA_WIDTH + 5 * HG_WIDTH + 2 * D_MODEL
ALPHA = (2 * DEPTH) ** 0.25
LN_EPS = 1e-5
RMS_EPS = 1e-6
NEG_BIG = -1e30
F_MIN = 1e-30

GATE_WIDTH = 2 * D_MODEL
HB_WIDTH = 5 * HG_WIDTH
HG_LEVELS = (32, 16, 8, 4, 2, 1)
HG_GROUP = 4
VMEM_LIMIT = 48 * 1024 * 1024
RANK_TILE = 256
RANK_STEP = 4
SLOTS_PER_PASS = 64
WIN = SLOTS_PER_PASS + 16


def _cparams(*sem):
    return pltpu.CompilerParams(dimension_semantics=sem, vmem_limit_bytes=VMEM_LIMIT)


def _sigmoid(x):
    return 1.0 / (1.0 + jnp.exp(-x))


def _layer_norm(x, g, b):
    mu = jnp.mean(x, axis=-1, keepdims=True)
    xc = x - mu
    var = jnp.mean(xc * xc, axis=-1, keepdims=True)
    return xc * lax.rsqrt(var + LN_EPS) * g + b


def _dot(a, b):
    return jnp.dot(a, b, preferred_element_type=F32)


def _dot_nt(a, b):
    return lax.dot_general(a, b, (((1,), (1,)), ((), ())), preferred_element_type=F32)


def _dot_tn(a, b):
    return lax.dot_general(a, b, (((0,), (0,)), ((), ())), preferred_element_type=F32)


def _ln_kernel(x_ref, g_ref, b_ref, o_ref):
    o_ref[...] = _layer_norm(x_ref[...], g_ref[...], b_ref[...])


def _ln_call(x, g, b, tm=512):
    n, d = x.shape
    row = pl.BlockSpec((tm, d), lambda i: (i, 0))
    vec = pl.BlockSpec((1, d), lambda i: (0, 0))
    return pl.pallas_call(
        _ln_kernel, grid=(n // tm,), in_specs=[row, vec, vec], out_specs=row,
        out_shape=jax.ShapeDtypeStruct((n, d), F32), compiler_params=_cparams("parallel"),
        name="ln_in")(x, g.reshape(1, d), b.reshape(1, d))


def _inproj_kernel(x_ref, w_ref, qkv_ref, gt_ref, hb_ref):
    xb = x_ref[...].astype(BF16)
    cw = 512
    for c in range(IN_WIDTH // cw):
        acc = _dot(xb, w_ref[:, c * cw:(c + 1) * cw])
        o = c * cw
        if o < 3 * NA_WIDTH:
            if o < NA_WIDTH:
                acc = acc * (NA_HEAD_DIM ** -0.5)
            qkv_ref[:, o:o + cw] = acc.astype(BF16)
        elif o < 3 * NA_WIDTH + HB_WIDTH:
            o -= 3 * NA_WIDTH
            hb_ref[:, o:o + cw] = acc
        else:
            o -= 3 * NA_WIDTH + HB_WIDTH
            gt_ref[:, o:o + cw] = _sigmoid(acc).astype(BF16)


def _inproj_call(x, w, tm=256):
    n, d = x.shape
    return pl.pallas_call(
        _inproj_kernel, grid=(n // tm,),
        in_specs=[pl.BlockSpec((tm, d), lambda i: (i, 0)),
                  pl.BlockSpec((d, IN_WIDTH), lambda i: (0, 0), pipeline_mode=pl.Buffered(1))],
        out_specs=[pl.BlockSpec((tm, 3 * NA_WIDTH), lambda i: (i, 0)),
                   pl.BlockSpec((tm, GATE_WIDTH), lambda i: (i, 0)),
                   pl.BlockSpec((tm, HB_WIDTH), lambda i: (i, 0))],
        out_shape=[jax.ShapeDtypeStruct((n, 3 * NA_WIDTH), BF16),
                   jax.ShapeDtypeStruct((n, GATE_WIDTH), BF16),
                   jax.ShapeDtypeStruct((n, HB_WIDTH), F32)],
        compiler_params=_cparams("parallel"), name="in_proj")(x, w)


def _na_kernel(*refs, rows):
    q_ref, k_ref, v_ref = refs[:3]
    bias_refs = refs[3:3 + NA_ROWS]
    o_ref, s_ref = refs[3 + NA_ROWS:]
    r0 = pl.program_id(1) * NA_ROWS
    w0 = jnp.minimum(_na_start(r0, rows), rows - NA_WIN)
    lane = lax.broadcasted_iota(jnp.int32, (1, 128), 1)
    lo = lane < NA_HEAD_DIM
    hi = jnp.logical_not(lo)
    offs = [pl.multiple_of((_na_start(r0 + i, rows) - w0) * GRID_W, GRID_W) for i in range(NA_ROWS)]
    for i in range(NA_ROWS):
        qr = slice(i * GRID_W, (i + 1) * GRID_W)
        kr = pl.ds(offs[i], NA_KH * GRID_W)
        for p in range(NA_HEADS // 2):
            cs = slice(p * 128, (p + 1) * 128)
            q2 = q_ref[qr, cs]
            k2 = k_ref[kr, cs]
            for hh in range(2):
                qm = jnp.where(hi if hh else lo, q2, jnp.zeros_like(q2))
                s = _dot_nt(qm, k2) + bias_refs[i][2 * p + hh]
                s_ref[i * NA_HEADS + 2 * p + hh] = s - jnp.max(s, axis=-1, keepdims=True)
    for i in range(NA_ROWS):
        qr = slice(i * GRID_W, (i + 1) * GRID_W)
        kr = pl.ds(offs[i], NA_KH * GRID_W)
        for p in range(NA_HEADS // 2):
            cs = slice(p * 128, (p + 1) * 128)
            v2 = v_ref[kr, cs]
            res = []
            for hh in range(2):
                e = jnp.exp(s_ref[i * NA_HEADS + 2 * p + hh]).astype(BF16)
                va = jnp.where(hi if hh else lo, v2, jnp.ones_like(v2))
                res.append(_dot(e, va))
            o = jnp.where(lo, res[0], res[1])
            l = pltpu.roll(jnp.where(lo, res[1], res[0]), NA_HEAD_DIM, 1)
            o_ref[qr, cs] = (o / l).astype(BF16)


def _na_start(r, rows):
    return jnp.clip(r - NA_KH // 2, 0, rows - NA_KH)


def _na_call(qkv, bias, batch, rows):
    n = qkv.shape[0]
    assert rows >= NA_WIN and rows % NA_ROWS == 0
    steps = rows // NA_ROWS

    def win_tok(b, j):
        w0 = jnp.minimum(_na_start(j * NA_ROWS, rows), rows - NA_WIN)
        return (b * rows + w0) * GRID_W

    def bias_spec(i):
        return pl.BlockSpec((None, NA_HEADS, GRID_W, NA_KH * GRID_W),
                            lambda b, j: (j * NA_ROWS + i - _na_start(j * NA_ROWS + i, rows), 0, 0, 0))

    win = (pl.Element(NA_WIN * GRID_W), pl.Element(NA_WIDTH))
    blk = (NA_ROWS * GRID_W, NA_WIDTH)
    return pl.pallas_call(
        functools.partial(_na_kernel, rows=rows), grid=(batch, steps),
        in_specs=[pl.BlockSpec(blk, lambda b, j: (b * steps + j, 0)),
                  pl.BlockSpec(win, lambda b, j: (win_tok(b, j), NA_WIDTH)),
                  pl.BlockSpec(win, lambda b, j: (win_tok(b, j), 2 * NA_WIDTH))]
                 + [bias_spec(i) for i in range(NA_ROWS)],
        out_specs=pl.BlockSpec(blk, lambda b, j: (b * steps + j, 0)),
        out_shape=jax.ShapeDtypeStruct((n, NA_WIDTH), BF16),
        scratch_shapes=[pltpu.VMEM((NA_ROWS * NA_HEADS, GRID_W, NA_KH * GRID_W), F32)],
        compiler_params=_cparams("parallel", "arbitrary"), name="na_attn")(
            qkv, qkv, qkv, *([bias] * NA_ROWS))


def _na_bias_tables(rpb):
    c = np.arange(GRID_W)
    c_start = np.clip(c - NA_KW // 2, 0, GRID_W - NA_KW)
    col_mask = (c[None, :] >= c_start[:, None]) & (c[None, :] < c_start[:, None] + NA_KW)
    dc = np.clip(c[None, :] - c[:, None], -(NA_KW - 1), NA_KW - 1) + (NA_KW - 1)
    onehot = jnp.asarray(dc[:, :, None] == np.arange(2 * NA_KW - 1), F32)
    t = jnp.einsum("lhab,qkb->lhaqk", rpb.astype(F32), onehot, precision=lax.Precision.HIGHEST)
    t = jnp.where(col_mask[None, None, None], t, NEG_BIG)
    per_cls = [t[:, :, NA_KH - 1 - cls:2 * NA_KH - 1 - cls] for cls in range(NA_KH)]
    t = jnp.stack(per_cls, axis=1)
    t = t.transpose(0, 1, 2, 4, 3, 5)
    return t.reshape(rpb.shape[0], NA_KH, NA_HEADS, GRID_W, NA_KH * GRID_W)


def _hg_constants(reverse):
    n = HG_CHUNK
    t = np.arange(n)
    if reverse:
        tri = (t[None, :] >= t[:, None])
    else:
        tri = (t[None, :] <= t[:, None])
    masks = []
    for c in HG_LEVELS:
        blk = t // (2 * c)
        upper = (t % (2 * c)) >= c
        same = blk[:, None] == blk[None, :]
        if reverse:
            m = same & (~upper)[:, None] & upper[None, :]
        else:
            m = same & upper[:, None] & (~upper)[None, :]
        masks.append(m)
    masks.append(np.eye(n, dtype=bool))
    tri = np.kron(np.eye(HG_GROUP, dtype=bool), tri)
    return jnp.asarray(tri, BF16), jnp.asarray(np.stack(masks), F32)


def _hgrn_kernel(*refs, reverse, final):
    if final:
        q_ref, z_ref, v_ref, lb_ref, tri_ref, lm_ref, gate_ref, prev_ref, ng_ref, o_ref, st_ref = refs
    else:
        q_ref, z_ref, v_ref, lb_ref, tri_ref, lm_ref, o_ref, st_ref = refs
    n = HG_CHUNK
    grp = HG_GROUP
    w = HG_WIDTH
    m = grp * n

    @pl.when(pl.program_id(1) == 0)
    def _():
        st_ref[...] = jnp.zeros_like(st_ref)

    z = z_ref[...]
    lb = lb_ref[...]
    f = lb + (1.0 - lb) * _sigmoid(z)
    g = jnp.log(jnp.maximum(f, F_MIN))
    kk = (1.0 - lb) * _sigmoid(-z)
    qv = q_ref[...]
    qq = qv * _sigmoid(qv)
    v_b = v_ref[...].astype(BF16)

    g1 = g.astype(BF16)
    r1 = g - g1.astype(F32)
    g2 = r1.astype(BF16)
    g3 = (r1 - g2.astype(F32)).astype(BF16)
    tri = tri_ref[...]
    b = _dot(tri, g1) + _dot(tri, g2) + _dot(tri, g3)
    last = 0 if reverse else n - 1
    bc = b.reshape(grp, n, w)
    b_last = bc[:, last:last + 1, :]
    e_last = jnp.exp(b_last)

    row = lax.broadcasted_iota(jnp.int32, (m, 1), 0)
    qs, ks = [], []
    for c in HG_LEVELS:
        ref_row = c if reverse else c - 1
        if c == 1:
            d = jnp.where((row % 2) == (0 if reverse else 1), g, 0.0)
        elif c == 2:
            b3 = b.reshape(m // 8, 8, w)
            sub = lax.broadcasted_iota(jnp.int32, (1, 8, 1), 1)
            r_lo = jnp.broadcast_to(b3[:, ref_row:ref_row + 1, :], b3.shape)
            r_hi = jnp.broadcast_to(b3[:, ref_row + 4:ref_row + 5, :], b3.shape)
            d = -jnp.abs(b3 - jnp.where(sub < 4, r_lo, r_hi)).reshape(m, w)
        else:
            b3 = b.reshape(m // (2 * c), 2 * c, w)
            d = -jnp.abs(b3 - jnp.broadcast_to(b3[:, ref_row:ref_row + 1, :], b3.shape)).reshape(m, w)
        e = jnp.exp(d)
        qs.append((qq * e).astype(BF16))
        ks.append((kk * e).astype(BF16))
    qs.append(qq.astype(BF16))
    ks.append(kk.astype(BF16))
    q_in = (qq * jnp.exp(b)).astype(BF16)
    k_out = (kk * jnp.exp(b_last - bc).reshape(m, w)).astype(BF16)

    intra = {}
    for ci in range(grp):
        rs = slice(ci * n, (ci + 1) * n)
        for h in range(HG_HEADS):
            hs = slice(h * HG_HEAD_DIM, (h + 1) * HG_HEAD_DIM)
            a = None
            for li in range(len(qs)):
                term = lm_ref[li] * _dot_nt(qs[li][rs, hs], ks[li][rs, hs])
                a = term if a is None else a + term
            intra[ci, h] = _dot(a.astype(BF16), v_b[rs, hs])

    if final:
        gv = gate_ref[...]
        gs = gv * _sigmoid(gv)
    for ci in (range(grp - 1, -1, -1) if reverse else range(grp)):
        rs = slice(ci * n, (ci + 1) * n)
        for h in range(HG_HEADS):
            hs = slice(h * HG_HEAD_DIM, (h + 1) * HG_HEAD_DIM)
            st = st_ref[h]
            o = intra[ci, h] + _dot_nt(q_in[rs, hs], st.astype(BF16))
            st_ref[h] = st * e_last[ci][:, hs] + _dot_tn(v_b[rs, hs], k_out[rs, hs])
            if final:
                o = o + prev_ref[rs, hs]
                o = o * lax.rsqrt(jnp.mean(o * o, axis=-1, keepdims=True) + RMS_EPS) * ng_ref[:, hs]
                o_ref[rs, hs] = (o * gs[rs, hs]).astype(BF16)
            else:
                o_ref[rs, hs] = o


def _hgrn_call(hb, lb, z_col, batch, seq, reverse, prev=None, norm_g=None):
    n = hb.shape[0]
    rows = HG_GROUP * HG_CHUNK
    assert seq % rows == 0
    nc = seq // rows
    final = prev is not None
    tri, masks = _hg_constants(reverse)

    def tok(b, c):
        return b * nc + ((nc - 1 - c) if reverse else c)

    def col(j):
        return pl.BlockSpec((rows, HG_WIDTH), lambda b, c: (tok(b, c), j))

    vec = pl.BlockSpec((1, HG_WIDTH), lambda b, c: (0, 0))
    in_specs = [col(0), col(z_col), col(3), vec,
                pl.BlockSpec((rows, rows), lambda b, c: (0, 0)),
                pl.BlockSpec(masks.shape, lambda b, c: (0, 0, 0))]
    args = [hb, hb, hb, lb.reshape(1, HG_WIDTH), tri, masks]
    if final:
        in_specs += [col(4), pl.BlockSpec((rows, HG_WIDTH), lambda b, c: (tok(b, c), 0)), vec]
        args += [hb, prev, norm_g.reshape(1, HG_WIDTH)]
    return pl.pallas_call(
        functools.partial(_hgrn_kernel, reverse=reverse, final=final),
        grid=(batch, nc), in_specs=in_specs,
        out_specs=pl.BlockSpec((rows, HG_WIDTH), lambda b, c: (tok(b, c), 0)),
        out_shape=jax.ShapeDtypeStruct((n, HG_WIDTH), BF16 if final else F32),
        scratch_shapes=[pltpu.VMEM((HG_HEADS, HG_HEAD_DIM, HG_HEAD_DIM), F32)],
        compiler_params=_cparams("arbitrary", "arbitrary"),
        name="hgrn_bwd" if reverse else "hgrn_fwd")(*args)


def _mix_kernel(na_ref, hg_ref, gt_ref, x_ref, wna_ref, whg_ref, wout_ref, g_ref, b_ref, wr_ref,
                x1_ref, x1b_ref, aff_ref):
    a = _dot(na_ref[...], wna_ref[...])
    r = _dot(hg_ref[...], whg_ref[...])
    mix = gt_ref[:, :D_MODEL].astype(F32) * a + gt_ref[:, D_MODEL:].astype(F32) * r
    y = _dot(mix.astype(BF16), wout_ref[...])
    x1 = _layer_norm(ALPHA * x_ref[...] + y, g_ref[...], b_ref[...])
    x1_ref[...] = x1
    x_hi = x1.astype(BF16)
    x1b_ref[...] = x_hi
    x_lo = (x1 - x_hi.astype(F32)).astype(BF16)
    p = _dot(x_hi, wr_ref[...]) + _dot(x_lo, wr_ref[...])
    logits = p[:, :N_EXPERTS] + p[:, N_EXPERTS:]
    e = jnp.exp(logits - jnp.max(logits, axis=-1, keepdims=True))
    aff_ref[...] = e / jnp.sum(e, axis=-1, keepdims=True)


def _mix_call(na, hg, gt, x, wna, whg, wout, g, b, wr2, tm=256):
    n, d = x.shape
    const = lambda shape: pl.BlockSpec(shape, lambda i: (0,) * len(shape))
    return pl.pallas_call(
        _mix_kernel, grid=(n // tm,),
        in_specs=[pl.BlockSpec((tm, NA_WIDTH), lambda i: (i, 0)),
                  pl.BlockSpec((tm, HG_WIDTH), lambda i: (i, 0)),
                  pl.BlockSpec((tm, 2 * D_MODEL), lambda i: (i, 0)),
                  pl.BlockSpec((tm, d), lambda i: (i, 0)),
                  const((NA_WIDTH, d)), const((HG_WIDTH, d)), const((d, d)),
                  const((1, d)), const((1, d)), const((d, 2 * N_EXPERTS))],
        out_specs=[pl.BlockSpec((tm, d), lambda i: (i, 0)),
                   pl.BlockSpec((tm, d), lambda i: (i, 0)),
                   pl.BlockSpec((tm, N_EXPERTS), lambda i: (i, 0))],
        out_shape=[jax.ShapeDtypeStruct((n, d), F32), jax.ShapeDtypeStruct((n, d), BF16),
                   jax.ShapeDtypeStruct((n, N_EXPERTS), F32)],
        compiler_params=_cparams("parallel"), name="mix_router")(
            na, hg, gt, x, wna, whg, wout, g.reshape(1, d), b.reshape(1, d), wr2)


def _ffn_kernel(xs_ref, wg_ref, wu_ref, wd_ref, gate_ref, o_ref, acc_ref):
    k = pl.program_id(2)
    xs = xs_ref[...]
    hg = _dot(xs, wg_ref[...])
    hu = _dot(xs, wu_ref[...])
    part = _dot((hg * _sigmoid(hg) * hu).astype(BF16), wd_ref[...])

    @pl.when(k == 0)
    def _():
        acc_ref[...] = part

    @pl.when(k > 0)
    def _():
        acc_ref[...] += part

    @pl.when(k == pl.num_programs(2) - 1)
    def _():
        o_ref[...] = (acc_ref[...] * gate_ref[...]).astype(BF16)


def _ffn_call(xs, wg, wu, wd, gate, tc=1024, tf=1024):
    e, c, d = xs.shape
    f = wg.shape[-1]
    tc = min(tc, c)
    return pl.pallas_call(
        _ffn_kernel, grid=(e, c // tc, f // tf),
        in_specs=[pl.BlockSpec((None, tc, d), lambda i, j, k: (i, j, 0)),
                  pl.BlockSpec((None, d, tf), lambda i, j, k: (i, 0, k)),
                  pl.BlockSpec((None, d, tf), lambda i, j, k: (i, 0, k)),
                  pl.BlockSpec((None, tf, d), lambda i, j, k: (i, k, 0)),
                  pl.BlockSpec((None, tc, 1), lambda i, j, k: (i, j, 0))],
        out_specs=pl.BlockSpec((None, tc, d), lambda i, j, k: (i, j, 0)),
        out_shape=jax.ShapeDtypeStruct((e, c, d), BF16),
        scratch_shapes=[pltpu.VMEM((tc, d), F32)],
        compiler_params=_cparams("parallel", "parallel", "arbitrary"), name="expert_ffn")(
            xs, wg, wu, wd, gate.reshape(e, c, 1))


def _rank_kernel(aff_ref, tau_ref, need_ref, lrank_ref, base_ref, ceq_ref, csel_ref):
    @pl.when(pl.program_id(0) == 0)
    def _():
        ceq_ref[...] = jnp.zeros_like(ceq_ref)
        csel_ref[...] = jnp.zeros_like(csel_ref)

    tau = tau_ref[:, 0:1]
    u = lax.broadcasted_iota(jnp.int32, (RANK_TILE, RANK_TILE), 0)
    t = lax.broadcasted_iota(jnp.int32, (RANK_TILE, RANK_TILE), 1)
    before = jnp.where(u < t, 1.0, 0.0).astype(BF16)
    ones = jnp.ones((RANK_TILE, 128), BF16)
    for i in range(RANK_STEP):
        ts = slice(i * RANK_TILE, (i + 1) * RANK_TILE)
        a = aff_ref[:, ts]
        eq = a == tau
        eq_f = jnp.where(eq, 1.0, 0.0).astype(BF16)
        eq_before = ceq_ref[:, 0:1] + _dot(eq_f, before)
        sel = jnp.logical_or(a > tau, jnp.logical_and(eq, eq_before < need_ref[:, 0:1]))
        sel_f = jnp.where(sel, 1.0, 0.0).astype(BF16)
        lrank_ref[:, ts] = jnp.where(sel, _dot(sel_f, before), -1.0).astype(jnp.int32)
        base_ref[i] = csel_ref[...].astype(jnp.int32)
        ceq_ref[...] += _dot(eq_f, ones)
        csel_ref[...] += _dot(sel_f, ones)


def _rank_call(aff_t, tau, need):
    e, n = aff_t.shape
    nt = n // RANK_TILE
    step = RANK_STEP * RANK_TILE
    assert n % step == 0
    rep = lambda v: jnp.broadcast_to(v.astype(F32).reshape(e, 1), (e, 128))
    vec = pl.BlockSpec((e, 128), lambda j: (0, 0))
    return pl.pallas_call(
        _rank_kernel, grid=(n // step,),
        in_specs=[pl.BlockSpec((e, step), lambda j: (0, j)), vec, vec],
        out_specs=[pl.BlockSpec((e, step), lambda j: (0, j)),
                   pl.BlockSpec((RANK_STEP, e, 128), lambda j: (j, 0, 0))],
        out_shape=[jax.ShapeDtypeStruct((e, n), jnp.int32),
                   jax.ShapeDtypeStruct((nt, e, 128), jnp.int32)],
        scratch_shapes=[pltpu.VMEM((e, 128), F32), pltpu.VMEM((e, 128), F32)],
        compiler_params=_cparams("arbitrary"), name="slot_rank")(aff_t, rep(tau), rep(need))


def _combine_kernel(base_ref, npass_ref, x_ref, lrank_ref, g_ref, b_ref, out_hbm, o_ref,
                    buf_ref, acc_ref, sem_ref, *, cap):
    j = pl.program_id(0)
    nt = pl.num_programs(0)

    def win_start(tile, e, p):
        r = base_ref[e * nt + tile] + p * SLOTS_PER_PASS
        return jnp.minimum((r // 16) * 16, cap - WIN)

    def window_copy(tile, e, p, slot):
        a = pl.multiple_of(win_start(tile, e, p), 16)
        return pltpu.make_async_copy(out_hbm.at[e, pl.ds(a, WIN), :],
                                     buf_ref.at[slot, pl.ds(e * WIN, WIN), :], sem_ref.at[slot])

    def start_all(tile, p, slot):
        for e in range(N_EXPERTS):
            window_copy(tile, e, p, slot).start()

    def wait_all(tile, p, slot):
        for e in range(N_EXPERTS):
            window_copy(tile, e, p, slot).wait()

    def add_pass(p, slot):
        row = lax.broadcasted_iota(jnp.int32, (WIN, RANK_TILE), 0)
        sel = []
        for e in range(N_EXPERTS):
            lr = lrank_ref[e:e + 1, :]
            off = base_ref[e * nt + j] - win_start(j, e, p)
            pos = jnp.where(jnp.logical_and(lr >= p * SLOTS_PER_PASS, lr < (p + 1) * SLOTS_PER_PASS),
                            lr + off, -1)
            sel.append(jnp.where(row == pos, 1.0, 0.0).astype(BF16))
        acc_ref[...] += _dot_tn(jnp.concatenate(sel, axis=0), buf_ref[slot])

    slot = j % 2

    @pl.when(j == 0)
    def _():
        start_all(0, 0, 0)

    @pl.when(j + 1 < nt)
    def _():
        start_all(j + 1, 0, 1 - slot)

    acc_ref[...] = jnp.zeros_like(acc_ref)
    wait_all(j, 0, slot)
    add_pass(0, slot)

    def extra(p, carry):
        start_all(j, p, slot)
        wait_all(j, p, slot)
        add_pass(p, slot)
        return carry

    lax.fori_loop(1, npass_ref[j], extra, 0)
    o_ref[...] = _layer_norm(ALPHA * x_ref[...] + acc_ref[...], g_ref[...], b_ref[...])


def _combine_call(x1, lrank, base, out, g, b):
    n, d = x1.shape
    e, cap, _ = out.shape
    nt = n // RANK_TILE
    assert cap >= WIN and cap % 16 == 0
    cnt = jnp.concatenate([base[:, 1:], jnp.full((e, 1), cap, jnp.int32)], axis=1) - base
    npass = jnp.maximum(1, (jnp.max(cnt, axis=0) + SLOTS_PER_PASS - 1) // SLOTS_PER_PASS).astype(jnp.int32)
    vec = pl.BlockSpec((1, d), lambda j, *_: (0, 0))
    grid_spec = pltpu.PrefetchScalarGridSpec(
        num_scalar_prefetch=2, grid=(nt,),
        in_specs=[pl.BlockSpec((RANK_TILE, d), lambda j, *_: (j, 0)),
                  pl.BlockSpec((e, RANK_TILE), lambda j, *_: (0, j)),
                  vec, vec, pl.BlockSpec(memory_space=pl.ANY)],
        out_specs=pl.BlockSpec((RANK_TILE, d), lambda j, *_: (j, 0)),
        scratch_shapes=[pltpu.VMEM((2, e * WIN, d), BF16), pltpu.VMEM((RANK_TILE, d), F32),
                        pltpu.SemaphoreType.DMA((2,))])
    return pl.pallas_call(
        functools.partial(_combine_kernel, cap=cap), grid_spec=grid_spec,
        out_shape=jax.ShapeDtypeStruct((n, d), F32),
        compiler_params=_cparams("arbitrary"), name="combine_ln")(
            base.reshape(-1), npass, x1, lrank, g.reshape(1, d), b.reshape(1, d), out)


def _hi_lo(w):
    hi = w.astype(BF16)
    lo = (w - hi.astype(F32)).astype(BF16)
    return jnp.concatenate([hi, lo], axis=-1)


def _lower_bounds(p):
    sm = jax.nn.softmax(p.astype(F32), axis=0)
    return jnp.cumsum(sm, axis=0) - sm[0:1]


def _trunk(x, prm):
    batch, seq, d = x.shape
    n = batch * seq
    rows = seq // GRID_W
    cap = EC_FACTOR * n // N_EXPERTS
    x = _ln_call(x.reshape(n, d), prm["ln_in_g"], prm["ln_in_b"])
    for l in range(DEPTH):
        qkv, gt, hb = _inproj_call(x, prm["w_in"][l])
        na = _na_call(qkv, prm["na_bias"][l], batch, rows)
        o_f = _hgrn_call(hb, prm["lb_fwd"][l], 1, batch, seq, reverse=False)
        hg = _hgrn_call(hb, prm["lb_bwd"][l], 2, batch, seq, reverse=True, prev=o_f,
                        norm_g=prm["hg_norm_g"][l])
        x1, x1b, aff = _mix_call(na, hg, gt, x, prm["w_branch_na"][l], prm["w_branch_hg"][l],
                                 prm["w_out"][l], prm["ln1_g"][l], prm["ln1_b"][l], prm["w_router2"][l])
        aff_t = aff.T
        gate, idx = lax.top_k(aff_t, cap)
        idx, gate = lax.sort((idx, gate), dimension=1, num_keys=1)
        tau = gate.min(axis=1)
        need = cap - jnp.sum(aff_t > tau[:, None], axis=1)
        lrank, base = _rank_call(aff_t, tau, need)
        xs = x1b[idx]
        out = _ffn_call(xs, prm["w_gate_e"][l], prm["w_up_e"][l], prm["w_down_e"][l], gate)
        x = _combine_call(x1, lrank, base[:, :, 0].T, out, prm["ln2_g"][l], prm["ln2_b"][l])
    return x.reshape(batch, seq, d)


@jax.jit
def kernel(x_prompt, x_sample, ln_in_g, ln_in_b, w_in, na_rpb, hg_lb_fwd, hg_lb_bwd, hg_norm_g,
           w_branch_na, w_branch_hg, w_out, ln1_g, ln1_b, w_router, w_gate_e, w_up_e, w_down_e,
           ln2_g, ln2_b):
    prm = dict(
        ln_in_g=ln_in_g, ln_in_b=ln_in_b, w_in=w_in.astype(BF16), na_bias=_na_bias_tables(na_rpb),
        lb_fwd=_lower_bounds(hg_lb_fwd), lb_bwd=_lower_bounds(hg_lb_bwd), hg_norm_g=hg_norm_g,
        w_branch_na=w_branch_na.astype(BF16), w_branch_hg=w_branch_hg.astype(BF16),
        w_out=w_out.astype(BF16), ln1_g=ln1_g, ln1_b=ln1_b,
        w_router2=_hi_lo(w_router),
        w_gate_e=w_gate_e.astype(BF16), w_up_e=w_up_e.astype(BF16), w_down_e=w_down_e.astype(BF16),
        ln2_g=ln2_g, ln2_b=ln2_b)
    return _trunk(x_prompt, prm), _trunk(x_sample, prm)
```

```python
import functools

import jax
import jax.numpy as jnp
import numpy as np
from jax import lax
from jax.experimental import pallas as pl
from jax.experimental.pallas import tpu as pltpu

F32 = jnp.float32
BF16 = jnp.bfloat16

D_MODEL = 1024
DEPTH = 4
GRID_W = 64
NA_HEADS = 8
NA_HEAD_DIM = 64
NA_WIDTH = NA_HEADS * NA_HEAD_DIM
NA_KH = 8
NA_KW = 16
NA_ROWS = 4
NA_WIN = NA_KH + NA_ROWS - 1
HG_HEADS = 4
HG_HEAD_DIM = 128
HG_WIDTH = HG_HEADS * HG_HEAD_DIM
HG_CHUNK = 64
N_EXPERTS = 16
EC_FACTOR = 2
D_FF = 2 * D_MODEL
IN_WIDTH = 3 * NA_WIDTH + 5 * HG_WIDTH + 2 * D_MODEL
ALPHA = (2 * DEPTH) ** 0.25
LN_EPS = 1e-5
RMS_EPS = 1e-6
NEG_BIG = -1e30
F_MIN = 1e-30

GATE_WIDTH = 2 * D_MODEL
HF_WIDTH = 3 * HG_WIDTH
HK_WIDTH = 4 * HG_WIDTH
INPROJ_CHUNK = 256
HG_LEVELS = (32, 16, 8, 4, 2, 1)
HG_GROUP = 4
VMEM_LIMIT = 48 * 1024 * 1024
RANK_TILE = 256
RANK_STEP = 4
SLOTS_PER_PASS = 48
WIN = SLOTS_PER_PASS + 16


def _cparams(*sem):
    return pltpu.CompilerParams(dimension_semantics=sem, vmem_limit_bytes=VMEM_LIMIT)


def _sigmoid(x):
    return 1.0 / (1.0 + jnp.exp(-x))


def _layer_norm(x, g, b):
    mu = jnp.mean(x, axis=-1, keepdims=True)
    xc = x - mu
    var = jnp.mean(xc * xc, axis=-1, keepdims=True)
    return xc * lax.rsqrt(var + LN_EPS) * g + b


def _dot(a, b):
    return jnp.dot(a, b, preferred_element_type=F32)


def _dot_nt(a, b):
    return lax.dot_general(a, b, (((1,), (1,)), ((), ())), preferred_element_type=F32)


def _dot_tn(a, b):
    return lax.dot_general(a, b, (((0,), (0,)), ((), ())), preferred_element_type=F32)


def _ln_kernel(x_ref, g_ref, b_ref, o_ref):
    o_ref[...] = _layer_norm(x_ref[...], g_ref[...], b_ref[...])


def _ln_call(x, g, b, tm=512):
    n, d = x.shape
    row = pl.BlockSpec((tm, d), lambda i: (i, 0))
    vec = pl.BlockSpec((1, d), lambda i: (0, 0))
    return pl.pallas_call(
        _ln_kernel, grid=(n // tm,), in_specs=[row, vec, vec], out_specs=row,
        out_shape=jax.ShapeDtypeStruct((n, d), F32), compiler_params=_cparams("parallel"),
        name="ln_in")(x, g.reshape(1, d), b.reshape(1, d))


def _inproj_kernel(x_ref, w_ref, lbf_ref, lbb_ref, qkv_ref, gt_ref, hf_ref, hk_ref):
    xb = x_ref[...].astype(BF16)
    cw = INPROJ_CHUNK
    for c in range(IN_WIDTH // cw):
        acc = _dot(xb, w_ref[:, c * cw:(c + 1) * cw])
        o = c * cw
        if o < 3 * NA_WIDTH:
            if o < NA_WIDTH:
                acc = acc * (NA_HEAD_DIM ** -0.5)
            qkv_ref[:, o:o + cw] = acc.astype(BF16)
        elif o < 3 * NA_WIDTH + 5 * HG_WIDTH:
            o -= 3 * NA_WIDTH
            s, o = o // HG_WIDTH, o % HG_WIDTH
            if s == 0:
                hf_ref[:, o:o + cw] = acc * _sigmoid(acc)
            elif s in (1, 2):
                lb = (lbf_ref if s == 1 else lbb_ref)[:, o:o + cw]
                f = lb + (1.0 - lb) * _sigmoid(acc)
                hf_ref[:, s * HG_WIDTH + o:s * HG_WIDTH + o + cw] = jnp.log(jnp.maximum(f, F_MIN))
                hk_ref[:, (s - 1) * HG_WIDTH + o:(s - 1) * HG_WIDTH + o + cw] = (
                    (1.0 - lb) * _sigmoid(-acc)).astype(BF16)
            elif s == 3:
                hk_ref[:, 2 * HG_WIDTH + o:2 * HG_WIDTH + o + cw] = acc.astype(BF16)
            else:
                hk_ref[:, 3 * HG_WIDTH + o:3 * HG_WIDTH + o + cw] = (acc * _sigmoid(acc)).astype(BF16)
        else:
            o -= 3 * NA_WIDTH + 5 * HG_WIDTH
            gt_ref[:, o:o + cw] = _sigmoid(acc).astype(BF16)


def _inproj_call(x, w, lb_fwd, lb_bwd, tm=256):
    n, d = x.shape
    vec = pl.BlockSpec((1, HG_WIDTH), lambda i: (0, 0))
    return pl.pallas_call(
        _inproj_kernel, grid=(n // tm,),
        in_specs=[pl.BlockSpec((tm, d), lambda i: (i, 0)),
                  pl.BlockSpec((d, IN_WIDTH), lambda i: (0, 0), pipeline_mode=pl.Buffered(1)), vec, vec],
        out_specs=[pl.BlockSpec((tm, 3 * NA_WIDTH), lambda i: (i, 0)),
                   pl.BlockSpec((tm, GATE_WIDTH), lambda i: (i, 0)),
                   pl.BlockSpec((tm, HF_WIDTH), lambda i: (i, 0)),
                   pl.BlockSpec((tm, HK_WIDTH), lambda i: (i, 0))],
        out_shape=[jax.ShapeDtypeStruct((n, 3 * NA_WIDTH), BF16),
                   jax.ShapeDtypeStruct((n, GATE_WIDTH), BF16),
                   jax.ShapeDtypeStruct((n, HF_WIDTH), F32),
                   jax.ShapeDtypeStruct((n, HK_WIDTH), BF16)],
        compiler_params=_cparams("parallel"), name="in_proj")(
            x, w, lb_fwd.reshape(1, HG_WIDTH), lb_bwd.reshape(1, HG_WIDTH))


def _na_kernel(*refs, rows):
    q_ref, k_ref, v_ref = refs[:3]
    bias_refs = refs[3:3 + NA_ROWS]
    o_ref, s_ref = refs[3 + NA_ROWS:]
    r0 = pl.program_id(1) * NA_ROWS
    w0 = jnp.minimum(_na_start(r0, rows), rows - NA_WIN)
    lane = lax.broadcasted_iota(jnp.int32, (1, 128), 1)
    lo = lane < NA_HEAD_DIM
    hi = jnp.logical_not(lo)
    offs = [pl.multiple_of((_na_start(r0 + i, rows) - w0) * GRID_W, GRID_W) for i in range(NA_ROWS)]
    for i in range(NA_ROWS):
        qr = slice(i * GRID_W, (i + 1) * GRID_W)
        kr = pl.ds(offs[i], NA_KH * GRID_W)
        for p in range(NA_HEADS // 2):
            cs = slice(p * 128, (p + 1) * 128)
            q2 = q_ref[qr, cs]
            k2 = k_ref[kr, cs]
            for hh in range(2):
                qm = jnp.where(hi if hh else lo, q2, jnp.zeros_like(q2))
                s = _dot_nt(qm, k2) + bias_refs[i][2 * p + hh]
                s_ref[i * NA_HEADS + 2 * p + hh] = s - jnp.max(s, axis=-1, keepdims=True)
    for i in range(NA_ROWS):
        qr = slice(i * GRID_W, (i + 1) * GRID_W)
        kr = pl.ds(offs[i], NA_KH * GRID_W)
        for p in range(NA_HEADS // 2):
            cs = slice(p * 128, (p + 1) * 128)
            v2 = v_ref[kr, cs]
            res = []
            for hh in range(2):
                e = jnp.exp(s_ref[i * NA_HEADS + 2 * p + hh]).astype(BF16)
                va = jnp.where(hi if hh else lo, v2, jnp.ones_like(v2))
                res.append(_dot(e, va))
            o = jnp.where(lo, res[0], res[1])
            l = pltpu.roll(jnp.where(lo, res[1], res[0]), NA_HEAD_DIM, 1)
            o_ref[qr, cs] = (o / l).astype(BF16)


def _na_start(r, rows):
    return jnp.clip(r - NA_KH // 2, 0, rows - NA_KH)


def _na_call(qkv, bias, batch, rows):
    n = qkv.shape[0]
    assert rows >= NA_WIN and rows % NA_ROWS == 0
    steps = rows // NA_ROWS

    def win_tok(b, j):
        w0 = jnp.minimum(_na_start(j * NA_ROWS, rows), rows - NA_WIN)
        return (b * rows + w0) * GRID_W

    def bias_spec(i):
        return pl.BlockSpec((None, NA_HEADS, GRID_W, NA_KH * GRID_W),
                            lambda b, j: (j * NA_ROWS + i - _na_start(j * NA_ROWS + i, rows), 0, 0, 0))

    win = (pl.Element(NA_WIN * GRID_W), pl.Element(NA_WIDTH))
    blk = (NA_ROWS * GRID_W, NA_WIDTH)
    return pl.pallas_call(
        functools.partial(_na_kernel, rows=rows), grid=(batch, steps),
        in_specs=[pl.BlockSpec(blk, lambda b, j: (b * steps + j, 0)),
                  pl.BlockSpec(win, lambda b, j: (win_tok(b, j), NA_WIDTH)),
                  pl.BlockSpec(win, lambda b, j: (win_tok(b, j), 2 * NA_WIDTH))]
                 + [bias_spec(i) for i in range(NA_ROWS)],
        out_specs=pl.BlockSpec(blk, lambda b, j: (b * steps + j, 0)),
        out_shape=jax.ShapeDtypeStruct((n, NA_WIDTH), BF16),
        scratch_shapes=[pltpu.VMEM((NA_ROWS * NA_HEADS, GRID_W, NA_KH * GRID_W), F32)],
        compiler_params=_cparams("parallel", "arbitrary"), name="na_attn")(
            qkv, qkv, qkv, *([bias] * NA_ROWS))


def _na_bias_tables(rpb):
    c = np.arange(GRID_W)
    c_start = np.clip(c - NA_KW // 2, 0, GRID_W - NA_KW)
    col_mask = (c[None, :] >= c_start[:, None]) & (c[None, :] < c_start[:, None] + NA_KW)
    dc = np.clip(c[None, :] - c[:, None], -(NA_KW - 1), NA_KW - 1) + (NA_KW - 1)
    onehot = jnp.asarray(dc[:, :, None] == np.arange(2 * NA_KW - 1), F32)
    t = jnp.einsum("lhab,qkb->lhqak", rpb.astype(F32), onehot, precision=lax.Precision.HIGHEST)
    t = jnp.where(col_mask[None, None, :, None, :], t, NEG_BIG)
    per_cls = [t[:, :, :, NA_KH - 1 - cls:2 * NA_KH - 1 - cls] for cls in range(NA_KH)]
    t = jnp.stack(per_cls, axis=1)
    return t.reshape(rpb.shape[0], NA_KH, NA_HEADS, GRID_W, NA_KH * GRID_W)


def _hg_constants(reverse):
    n = HG_CHUNK
    t = np.arange(n)
    if reverse:
        tri = (t[None, :] >= t[:, None])
    else:
        tri = (t[None, :] <= t[:, None])
    masks = []
    for c in HG_LEVELS:
        blk = t // (2 * c)
        upper = (t % (2 * c)) >= c
        same = blk[:, None] == blk[None, :]
        if reverse:
            m = same & (~upper)[:, None] & upper[None, :]
        else:
            m = same & upper[:, None] & (~upper)[None, :]
        masks.append(m)
    masks.append(np.eye(n, dtype=bool))
    tri = np.kron(np.eye(HG_GROUP, dtype=bool), tri)
    return jnp.asarray(tri, BF16), jnp.asarray(np.stack(masks), F32)


def _hgrn_kernel(*refs, reverse, final):
    if final:
        q_ref, g_ref, k_ref, v_ref, tri_ref, lm_ref, gate_ref, prev_ref, ng_ref, o_ref, st_ref = refs
    else:
        q_ref, g_ref, k_ref, v_ref, tri_ref, lm_ref, o_ref, st_ref = refs
    n = HG_CHUNK
    grp = HG_GROUP
    w = HG_WIDTH
    m = grp * n

    @pl.when(pl.program_id(1) == 0)
    def _():
        st_ref[...] = jnp.zeros_like(st_ref)

    g = g_ref[...]
    kk = k_ref[...].astype(F32)
    qq = q_ref[...]
    v_b = v_ref[...]

    g1 = g.astype(BF16)
    r1 = g - g1.astype(F32)
    g2 = r1.astype(BF16)
    g3 = (r1 - g2.astype(F32)).astype(BF16)
    tri = tri_ref[...]
    b = _dot(tri, g1) + _dot(tri, g2) + _dot(tri, g3)
    last = 0 if reverse else n - 1
    bc = b.reshape(grp, n, w)
    b_last = bc[:, last:last + 1, :]
    e_last = jnp.exp(b_last)

    row = lax.broadcasted_iota(jnp.int32, (m, 1), 0)
    qs, ks = [], []
    for c in HG_LEVELS:
        ref_row = c if reverse else c - 1
        if c == 1:
            d = jnp.where((row % 2) == (0 if reverse else 1), g, 0.0)
        elif c == 2:
            b3 = b.reshape(m // 8, 8, w)
            sub = lax.broadcasted_iota(jnp.int32, (1, 8, 1), 1)
            r_lo = jnp.broadcast_to(b3[:, ref_row:ref_row + 1, :], b3.shape)
            r_hi = jnp.broadcast_to(b3[:, ref_row + 4:ref_row + 5, :], b3.shape)
            d = -jnp.abs(b3 - jnp.where(sub < 4, r_lo, r_hi)).reshape(m, w)
        else:
            b3 = b.reshape(m // (2 * c), 2 * c, w)
            d = -jnp.abs(b3 - jnp.broadcast_to(b3[:, ref_row:ref_row + 1, :], b3.shape)).reshape(m, w)
        e = jnp.exp(d)
        qs.append((qq * e).astype(BF16))
        ks.append((kk * e).astype(BF16))
    qs.append(qq.astype(BF16))
    ks.append(kk.astype(BF16))
    q_in = (qq * jnp.exp(b)).astype(BF16)
    k_out = (kk * jnp.exp(b_last - bc).reshape(m, w)).astype(BF16)

    intra = {}
    for ci in range(grp):
        rs = slice(ci * n, (ci + 1) * n)
        for h in range(HG_HEADS):
            hs = slice(h * HG_HEAD_DIM, (h + 1) * HG_HEAD_DIM)
            a = None
            for li in range(len(qs)):
                term = lm_ref[li] * _dot_nt(qs[li][rs, hs], ks[li][rs, hs])
                a = term if a is None else a + term
            intra[ci, h] = _dot(a.astype(BF16), v_b[rs, hs])

    if final:
        gs = gate_ref[...].astype(F32)
    for ci in (range(grp - 1, -1, -1) if reverse else range(grp)):
        rs = slice(ci * n, (ci + 1) * n)
        for h in range(HG_HEADS):
            hs = slice(h * HG_HEAD_DIM, (h + 1) * HG_HEAD_DIM)
            st = st_ref[h]
            o = intra[ci, h] + _dot_nt(q_in[rs, hs], st.astype(BF16))
            st_ref[h] = st * e_last[ci][:, hs] + _dot_tn(v_b[rs, hs], k_out[rs, hs])
            if final:
                o = o + prev_ref[rs, hs]
                o = o * lax.rsqrt(jnp.mean(o * o, axis=-1, keepdims=True) + RMS_EPS) * ng_ref[:, hs]
                o_ref[rs, hs] = (o * gs[rs, hs]).astype(BF16)
            else:
                o_ref[rs, hs] = o


def _hgrn_call(hf, hk, batch, seq, reverse, prev=None, norm_g=None):
    n = hf.shape[0]
    rows = HG_GROUP * HG_CHUNK
    assert seq % rows == 0
    nc = seq // rows
    final = prev is not None
    tri, masks = _hg_constants(reverse)

    def tok(b, c):
        return b * nc + ((nc - 1 - c) if reverse else c)

    def col(j):
        return pl.BlockSpec((rows, HG_WIDTH), lambda b, c: (tok(b, c), j))

    vec = pl.BlockSpec((1, HG_WIDTH), lambda b, c: (0, 0))
    d = 1 if reverse else 0
    in_specs = [col(0), col(1 + d), col(d), col(2),
                pl.BlockSpec((rows, rows), lambda b, c: (0, 0)),
                pl.BlockSpec(masks.shape, lambda b, c: (0, 0, 0))]
    args = [hf, hf, hk, hk, tri, masks]
    if final:
        in_specs += [col(3), col(0), vec]
        args += [hk, prev, norm_g.reshape(1, HG_WIDTH)]
    return pl.pallas_call(
        functools.partial(_hgrn_kernel, reverse=reverse, final=final),
        grid=(batch, nc), in_specs=in_specs,
        out_specs=pl.BlockSpec((rows, HG_WIDTH), lambda b, c: (tok(b, c), 0)),
        out_shape=jax.ShapeDtypeStruct((n, HG_WIDTH), BF16 if final else F32),
        scratch_shapes=[pltpu.VMEM((HG_HEADS, HG_HEAD_DIM, HG_HEAD_DIM), F32)],
        compiler_params=_cparams("arbitrary", "arbitrary"),
        name="hgrn_bwd" if reverse else "hgrn_fwd")(*args)


def _mix_kernel(na_ref, hg_ref, gt_ref, x_ref, wna_ref, whg_ref, wout_ref, g_ref, b_ref, wr_ref,
                x1_ref, x1b_ref, aff_ref):
    a = _dot(na_ref[...], wna_ref[...])
    r = _dot(hg_ref[...], whg_ref[...])
    mix = gt_ref[:, :D_MODEL].astype(F32) * a + gt_ref[:, D_MODEL:].astype(F32) * r
    y = _dot(mix.astype(BF16), wout_ref[...])
    x1 = _layer_norm(ALPHA * x_ref[...] + y, g_ref[...], b_ref[...])
    x1_ref[...] = x1
    x_hi = x1.astype(BF16)
    x1b_ref[...] = x_hi
    x_lo = (x1 - x_hi.astype(F32)).astype(BF16)
    p = _dot(x_hi, wr_ref[...]) + _dot(x_lo, wr_ref[...])
    logits = p[:, :N_EXPERTS] + p[:, N_EXPERTS:]
    e = jnp.exp(logits - jnp.max(logits, axis=-1, keepdims=True))
    aff_ref[...] = e / jnp.sum(e, axis=-1, keepdims=True)


def _mix_call(na, hg, gt, x, wna, whg, wout, g, b, wr2, tm=256):
    n, d = x.shape
    const = lambda shape: pl.BlockSpec(shape, lambda i: (0,) * len(shape))
    return pl.pallas_call(
        _mix_kernel, grid=(n // tm,),
        in_specs=[pl.BlockSpec((tm, NA_WIDTH), lambda i: (i, 0)),
                  pl.BlockSpec((tm, HG_WIDTH), lambda i: (i, 0)),
                  pl.BlockSpec((tm, 2 * D_MODEL), lambda i: (i, 0)),
                  pl.BlockSpec((tm, d), lambda i: (i, 0)),
                  const((NA_WIDTH, d)), const((HG_WIDTH, d)), const((d, d)),
                  const((1, d)), const((1, d)), const((d, 2 * N_EXPERTS))],
        out_specs=[pl.BlockSpec((tm, d), lambda i: (i, 0)),
                   pl.BlockSpec((tm, d), lambda i: (i, 0)),
                   pl.BlockSpec((tm, N_EXPERTS), lambda i: (i, 0))],
        out_shape=[jax.ShapeDtypeStruct((n, d), F32), jax.ShapeDtypeStruct((n, d), BF16),
                   jax.ShapeDtypeStruct((n, N_EXPERTS), F32)],
        compiler_params=_cparams("parallel"), name="mix_router")(
            na, hg, gt, x, wna, whg, wout, g.reshape(1, d), b.reshape(1, d), wr2)


def _ffn_kernel(xs_ref, wg_ref, wu_ref, wd_ref, gate_ref, o_ref, acc_ref):
    k = pl.program_id(2)
    xs = xs_ref[...]
    hg = _dot(xs, wg_ref[...])
    hu = _dot(xs, wu_ref[...])
    part = _dot((hg * _sigmoid(hg) * hu).astype(BF16), wd_ref[...])

    @pl.when(k == 0)
    def _():
        acc_ref[...] = part

    @pl.when(k > 0)
    def _():
        acc_ref[...] += part

    @pl.when(k == pl.num_programs(2) - 1)
    def _():
        o_ref[...] = (acc_ref[...] * gate_ref[...]).astype(BF16)


def _ffn_call(xs, wg, wu, wd, gate, tc=1024, tf=1024):
    e, c, d = xs.shape
    f = wg.shape[-1]
    tc = min(tc, c)
    return pl.pallas_call(
        _ffn_kernel, grid=(e, c // tc, f // tf),
        in_specs=[pl.BlockSpec((None, tc, d), lambda i, j, k: (i, j, 0)),
                  pl.BlockSpec((None, d, tf), lambda i, j, k: (i, 0, k)),
                  pl.BlockSpec((None, d, tf), lambda i, j, k: (i, 0, k)),
                  pl.BlockSpec((None, tf, d), lambda i, j, k: (i, k, 0)),
                  pl.BlockSpec((None, tc, 1), lambda i, j, k: (i, j, 0))],
        out_specs=pl.BlockSpec((None, tc, d), lambda i, j, k: (i, j, 0)),
        out_shape=jax.ShapeDtypeStruct((e, c, d), BF16),
        scratch_shapes=[pltpu.VMEM((tc, d), F32)],
        compiler_params=_cparams("parallel", "parallel", "arbitrary"), name="expert_ffn")(
            xs, wg, wu, wd, gate.reshape(e, c, 1))


def _rank_kernel(aff_ref, tau_ref, need_ref, lrank_ref, base_ref, ceq_ref, csel_ref):
    @pl.when(pl.program_id(0) == 0)
    def _():
        ceq_ref[...] = jnp.zeros_like(ceq_ref)
        csel_ref[...] = jnp.zeros_like(csel_ref)

    tau = tau_ref[:, 0:1]
    u = lax.broadcasted_iota(jnp.int32, (RANK_TILE, RANK_TILE), 0)
    t = lax.broadcasted_iota(jnp.int32, (RANK_TILE, RANK_TILE), 1)
    before = jnp.where(u < t, 1.0, 0.0).astype(BF16)
    ones = jnp.ones((RANK_TILE, 128), BF16)
    for i in range(RANK_STEP):
        ts = slice(i * RANK_TILE, (i + 1) * RANK_TILE)
        a = aff_ref[:, ts]
        eq = a == tau
        eq_f = jnp.where(eq, 1.0, 0.0).astype(BF16)
        eq_before = ceq_ref[:, 0:1] + _dot(eq_f, before)
        sel = jnp.logical_or(a > tau, jnp.logical_and(eq, eq_before < need_ref[:, 0:1]))
        sel_f = jnp.where(sel, 1.0, 0.0).astype(BF16)
        lrank_ref[:, ts] = jnp.where(sel, _dot(sel_f, before), -1.0).astype(jnp.int32)
        base_ref[i] = csel_ref[...].astype(jnp.int32)
        ceq_ref[...] += _dot(eq_f, ones)
        csel_ref[...] += _dot(sel_f, ones)


def _rank_call(aff_t, tau, need):
    e, n = aff_t.shape
    nt = n // RANK_TILE
    step = RANK_STEP * RANK_TILE
    assert n % step == 0
    rep = lambda v: jnp.broadcast_to(v.astype(F32).reshape(e, 1), (e, 128))
    vec = pl.BlockSpec((e, 128), lambda j: (0, 0))
    return pl.pallas_call(
        _rank_kernel, grid=(n // step,),
        in_specs=[pl.BlockSpec((e, step), lambda j: (0, j)), vec, vec],
        out_specs=[pl.BlockSpec((e, step), lambda j: (0, j)),
                   pl.BlockSpec((RANK_STEP, e, 128), lambda j: (j, 0, 0))],
        out_shape=[jax.ShapeDtypeStruct((e, n), jnp.int32),
                   jax.ShapeDtypeStruct((nt, e, 128), jnp.int32)],
        scratch_shapes=[pltpu.VMEM((e, 128), F32), pltpu.VMEM((e, 128), F32)],
        compiler_params=_cparams("arbitrary"), name="slot_rank")(aff_t, rep(tau), rep(need))


def _combine_kernel(base_ref, npass_ref, x_ref, lrank_ref, g_ref, b_ref, out_hbm, o_ref,
                    buf_ref, acc_ref, sem_ref, *, cap):
    j = pl.program_id(0)
    nt = pl.num_programs(0)

    def win_start(tile, e, p):
        r = base_ref[e * nt + tile] + p * SLOTS_PER_PASS
        return jnp.minimum((r // 16) * 16, cap - WIN)

    def window_copy(tile, e, p, slot):
        a = pl.multiple_of(win_start(tile, e, p), 16)
        return pltpu.make_async_copy(out_hbm.at[e, pl.ds(a, WIN), :],
                                     buf_ref.at[slot, pl.ds(e * WIN, WIN), :], sem_ref.at[slot])

    def start_all(tile, p, slot):
        for e in range(N_EXPERTS):
            window_copy(tile, e, p, slot).start()

    def wait_all(tile, p, slot):
        for e in range(N_EXPERTS):
            window_copy(tile, e, p, slot).wait()

    def add_pass(p, slot):
        row = lax.broadcasted_iota(jnp.int32, (WIN, RANK_TILE), 0)
        sel = []
        for e in range(N_EXPERTS):
            lr = lrank_ref[e:e + 1, :]
            off = base_ref[e * nt + j] - win_start(j, e, p)
            pos = jnp.where(jnp.logical_and(lr >= p * SLOTS_PER_PASS, lr < (p + 1) * SLOTS_PER_PASS),
                            lr + off, -1)
            sel.append(jnp.where(row == pos, 1.0, 0.0).astype(BF16))
        acc_ref[...] += _dot_tn(jnp.concatenate(sel, axis=0), buf_ref[slot])

    slot = j % 2

    @pl.when(j == 0)
    def _():
        start_all(0, 0, 0)

    @pl.when(j + 1 < nt)
    def _():
        start_all(j + 1, 0, 1 - slot)

    acc_ref[...] = jnp.zeros_like(acc_ref)
    wait_all(j, 0, slot)
    add_pass(0, slot)

    def extra(p, carry):
        start_all(j, p, slot)
        wait_all(j, p, slot)
        add_pass(p, slot)
        return carry

    lax.fori_loop(1, npass_ref[j], extra, 0)
    o_ref[...] = _layer_norm(ALPHA * x_ref[...] + acc_ref[...], g_ref[...], b_ref[...])


def _combine_call(x1, lrank, base, out, g, b):
    n, d = x1.shape
    e, cap, _ = out.shape
    nt = n // RANK_TILE
    assert cap >= WIN and cap % 16 == 0
    cnt = jnp.concatenate([base[:, 1:], jnp.full((e, 1), cap, jnp.int32)], axis=1) - base
    npass = jnp.maximum(1, (jnp.max(cnt, axis=0) + SLOTS_PER_PASS - 1) // SLOTS_PER_PASS).astype(jnp.int32)
    vec = pl.BlockSpec((1, d), lambda j, *_: (0, 0))
    grid_spec = pltpu.PrefetchScalarGridSpec(
        num_scalar_prefetch=2, grid=(nt,),
        in_specs=[pl.BlockSpec((RANK_TILE, d), lambda j, *_: (j, 0)),
                  pl.BlockSpec((e, RANK_TILE), lambda j, *_: (0, j)),
                  vec, vec, pl.BlockSpec(memory_space=pl.ANY)],
        out_specs=pl.BlockSpec((RANK_TILE, d), lambda j, *_: (j, 0)),
        scratch_shapes=[pltpu.VMEM((2, e * WIN, d), BF16), pltpu.VMEM((RANK_TILE, d), F32),
                        pltpu.SemaphoreType.DMA((2,))])
    return pl.pallas_call(
        functools.partial(_combine_kernel, cap=cap), grid_spec=grid_spec,
        out_shape=jax.ShapeDtypeStruct((n, d), F32),
        compiler_params=_cparams("arbitrary"), name="combine_ln")(
            base.reshape(-1), npass, x1, lrank, g.reshape(1, d), b.reshape(1, d), out)


def _hi_lo(w):
    hi = w.astype(BF16)
    lo = (w - hi.astype(F32)).astype(BF16)
    return jnp.concatenate([hi, lo], axis=-1)


def _lower_bounds(p):
    sm = jax.nn.softmax(p.astype(F32), axis=0)
    return jnp.cumsum(sm, axis=0) - sm[0:1]


def _trunk(x, prm):
    batch, seq, d = x.shape
    n = batch * seq
    rows = seq // GRID_W
    cap = EC_FACTOR * n // N_EXPERTS
    x = _ln_call(x.reshape(n, d), prm["ln_in_g"], prm["ln_in_b"])
    for l in range(DEPTH):
        qkv, gt, hf, hk = _inproj_call(x, prm["w_in"][l], prm["lb_fwd"][l], prm["lb_bwd"][l])
        na = _na_call(qkv, prm["na_bias"][l], batch, rows)
        o_f = _hgrn_call(hf, hk, batch, seq, reverse=False)
        hg = _hgrn_call(hf, hk, batch, seq, reverse=True, prev=o_f, norm_g=prm["hg_norm_g"][l])
        x1, x1b, aff = _mix_call(na, hg, gt, x, prm["w_branch_na"][l], prm["w_branch_hg"][l],
                                 prm["w_out"][l], prm["ln1_g"][l], prm["ln1_b"][l], prm["w_router2"][l])
        aff_t = aff.T
        gate, idx = lax.top_k(aff_t, cap)
        idx, gate = lax.sort((idx, gate), dimension=1, num_keys=1)
        tau = gate.min(axis=1)
        need = cap - jnp.sum(aff_t > tau[:, None], axis=1)
        lrank, base = _rank_call(aff_t, tau, need)
        xs = x1b[idx]
        out = _ffn_call(xs, prm["w_gate_e"][l], prm["w_up_e"][l], prm["w_down_e"][l], gate)
        x = _combine_call(x1, lrank, base[:, :, 0].T, out, prm["ln2_g"][l], prm["ln2_b"][l])
    return x.reshape(batch, seq, d)


@jax.jit
def kernel(x_prompt, x_sample, ln_in_g, ln_in_b, w_in, na_rpb, hg_lb_fwd, hg_lb_bwd, hg_norm_g,
           w_branch_na, w_branch_hg, w_out, ln1_g, ln1_b, w_router, w_gate_e, w_up_e, w_down_e,
           ln2_g, ln2_b):
    prm = dict(
        ln_in_g=ln_in_g, ln_in_b=ln_in_b, w_in=w_in.astype(BF16), na_bias=_na_bias_tables(na_rpb),
        lb_fwd=_lower_bounds(hg_lb_fwd), lb_bwd=_lower_bounds(hg_lb_bwd), hg_norm_g=hg_norm_g,
        w_branch_na=w_branch_na.astype(BF16), w_branch_hg=w_branch_hg.astype(BF16),
        w_out=w_out.astype(BF16), ln1_g=ln1_g, ln1_b=ln1_b,
        w_router2=_hi_lo(w_router),
        w_gate_e=w_gate_e.astype(BF16), w_up_e=w_up_e.astype(BF16), w_down_e=w_down_e.astype(BF16),
        ln2_g=ln2_g, ln2_b=ln2_b)
    return _trunk(x_prompt, prm), _trunk(x_sample, prm)
```

```python
import functools

import jax
import jax.numpy as jnp
import numpy as np
from jax import lax
from jax.experimental import pallas as pl
from jax.experimental.pallas import tpu as pltpu

F32 = jnp.float32
BF16 = jnp.bfloat16

D_MODEL = 1024
DEPTH = 4
GRID_W = 64
NA_HEADS = 8
NA_HEAD_DIM = 64
NA_WIDTH = NA_HEADS * NA_HEAD_DIM
NA_KH = 8
NA_KW = 16
NA_ROWS = 4
NA_WIN = NA_KH + NA_ROWS - 1
HG_HEADS = 4
HG_HEAD_DIM = 128
HG_WIDTH = HG_HEADS * HG_HEAD_DIM
HG_CHUNK = 64
N_EXPERTS = 16
EC_FACTOR = 2
D_FF = 2 * D_MODEL
IN_WIDTH = 3 * NA_WIDTH + 5 * HG_WIDTH + 2 * D_MODEL
ALPHA = (2 * DEPTH) ** 0.25
LN_EPS = 1e-5
RMS_EPS = 1e-6
NEG_BIG = -1e30
F_MIN = 1e-30

GATE_WIDTH = 2 * D_MODEL
HF_WIDTH = 3 * HG_WIDTH
HK_WIDTH = 4 * HG_WIDTH
MIX_SPLIT = 2
INPROJ_CHUNK = 256
HG_LEVELS = (32, 16, 8, 4, 2, 1)
HG_GROUP = 4
VMEM_LIMIT = 48 * 1024 * 1024
RANK_TILE = 256
RANK_STEP = 4
SLOTS_PER_PASS = 48
WIN = SLOTS_PER_PASS + 16


def _cparams(*sem):
    return pltpu.CompilerParams(dimension_semantics=sem, vmem_limit_bytes=VMEM_LIMIT)


def _sigmoid(x):
    return 0.5 * jnp.tanh(0.5 * x) + 0.5


def _layer_norm(x, g, b):
    mu = jnp.mean(x, axis=-1, keepdims=True)
    xc = x - mu
    var = jnp.mean(xc * xc, axis=-1, keepdims=True)
    return xc * lax.rsqrt(var + LN_EPS) * g + b


def _dot(a, b):
    return jnp.dot(a, b, preferred_element_type=F32)


def _dot_nt(a, b):
    return lax.dot_general(a, b, (((1,), (1,)), ((), ())), preferred_element_type=F32)


def _dot_tn(a, b):
    return lax.dot_general(a, b, (((0,), (0,)), ((), ())), preferred_element_type=F32)


def _ln_kernel(x_ref, g_ref, b_ref, o_ref):
    o_ref[...] = _layer_norm(x_ref[...], g_ref[...], b_ref[...])


def _ln_call(x, g, b, tm=512):
    n, d = x.shape
    row = pl.BlockSpec((tm, d), lambda i: (i, 0))
    vec = pl.BlockSpec((1, d), lambda i: (0, 0))
    return pl.pallas_call(
        _ln_kernel, grid=(n // tm,), in_specs=[row, vec, vec], out_specs=row,
        out_shape=jax.ShapeDtypeStruct((n, d), F32), compiler_params=_cparams("parallel"),
        name="ln_in")(x, g.reshape(1, d), b.reshape(1, d))


def _inproj_kernel(x_ref, w_ref, lbf_ref, lbb_ref, qkv_ref, gt_ref, hf_ref, hk_ref):
    xb = x_ref[...].astype(BF16)
    cw = INPROJ_CHUNK
    for c in range(IN_WIDTH // cw):
        acc = _dot(xb, w_ref[:, c * cw:(c + 1) * cw])
        o = c * cw
        if o < 3 * NA_WIDTH:
            if o < NA_WIDTH:
                acc = acc * (NA_HEAD_DIM ** -0.5)
            qkv_ref[:, o:o + cw] = acc.astype(BF16)
        elif o < 3 * NA_WIDTH + 5 * HG_WIDTH:
            o -= 3 * NA_WIDTH
            s, o = o // HG_WIDTH, o % HG_WIDTH
            if s == 0:
                hf_ref[:, o:o + cw] = acc * _sigmoid(acc)
            elif s in (1, 2):
                lb = (lbf_ref if s == 1 else lbb_ref)[:, o:o + cw]
                sg = _sigmoid(acc)
                f = lb + (1.0 - lb) * sg
                hf_ref[:, s * HG_WIDTH + o:s * HG_WIDTH + o + cw] = jnp.log(jnp.maximum(f, F_MIN))
                hk_ref[:, (s - 1) * HG_WIDTH + o:(s - 1) * HG_WIDTH + o + cw] = (
                    (1.0 - lb) * (1.0 - sg)).astype(BF16)
            elif s == 3:
                hk_ref[:, 2 * HG_WIDTH + o:2 * HG_WIDTH + o + cw] = acc.astype(BF16)
            else:
                hk_ref[:, 3 * HG_WIDTH + o:3 * HG_WIDTH + o + cw] = (acc * _sigmoid(acc)).astype(BF16)
        else:
            o -= 3 * NA_WIDTH + 5 * HG_WIDTH
            gt_ref[:, o:o + cw] = _sigmoid(acc).astype(BF16)


def _inproj_call(x, w, lb_fwd, lb_bwd, tm=256):
    n, d = x.shape
    vec = pl.BlockSpec((1, HG_WIDTH), lambda i: (0, 0))
    return pl.pallas_call(
        _inproj_kernel, grid=(n // tm,),
        in_specs=[pl.BlockSpec((tm, d), lambda i: (i, 0)),
                  pl.BlockSpec((d, IN_WIDTH), lambda i: (0, 0), pipeline_mode=pl.Buffered(1)), vec, vec],
        out_specs=[pl.BlockSpec((tm, 3 * NA_WIDTH), lambda i: (i, 0)),
                   pl.BlockSpec((tm, GATE_WIDTH), lambda i: (i, 0)),
                   pl.BlockSpec((tm, HF_WIDTH), lambda i: (i, 0)),
                   pl.BlockSpec((tm, HK_WIDTH), lambda i: (i, 0))],
        out_shape=[jax.ShapeDtypeStruct((n, 3 * NA_WIDTH), BF16),
                   jax.ShapeDtypeStruct((n, GATE_WIDTH), BF16),
                   jax.ShapeDtypeStruct((n, HF_WIDTH), F32),
                   jax.ShapeDtypeStruct((n, HK_WIDTH), BF16)],
        compiler_params=_cparams("parallel"), name="in_proj")(
            x, w, lb_fwd.reshape(1, HG_WIDTH), lb_bwd.reshape(1, HG_WIDTH))


def _na_kernel(*refs, rows):
    q_ref, k_ref, v_ref = refs[:3]
    bias_refs = refs[3:3 + NA_ROWS]
    o_ref, s_ref = refs[3 + NA_ROWS:]
    r0 = pl.program_id(1) * NA_ROWS
    w0 = jnp.minimum(_na_start(r0, rows), rows - NA_WIN)
    lane = lax.broadcasted_iota(jnp.int32, (1, 128), 1)
    lo = lane < NA_HEAD_DIM
    hi = jnp.logical_not(lo)
    offs = [pl.multiple_of((_na_start(r0 + i, rows) - w0) * GRID_W, GRID_W) for i in range(NA_ROWS)]
    for i in range(NA_ROWS):
        qr = slice(i * GRID_W, (i + 1) * GRID_W)
        kr = pl.ds(offs[i], NA_KH * GRID_W)
        for p in range(NA_HEADS // 2):
            cs = slice(p * 128, (p + 1) * 128)
            q2 = q_ref[qr, cs]
            k2 = k_ref[kr, cs]
            for hh in range(2):
                qm = jnp.where(hi if hh else lo, q2, jnp.zeros_like(q2))
                s = _dot_nt(qm, k2) + bias_refs[i][2 * p + hh]
                s_ref[i * NA_HEADS + 2 * p + hh] = s - jnp.max(s, axis=-1, keepdims=True)
    for i in range(NA_ROWS):
        qr = slice(i * GRID_W, (i + 1) * GRID_W)
        kr = pl.ds(offs[i], NA_KH * GRID_W)
        for p in range(NA_HEADS // 2):
            cs = slice(p * 128, (p + 1) * 128)
            v2 = v_ref[kr, cs]
            res = []
            for hh in range(2):
                e = jnp.exp(s_ref[i * NA_HEADS + 2 * p + hh]).astype(BF16)
                va = jnp.where(hi if hh else lo, v2, jnp.ones_like(v2))
                res.append(_dot(e, va))
            o = jnp.where(lo, res[0], res[1])
            l = pltpu.roll(jnp.where(lo, res[1], res[0]), NA_HEAD_DIM, 1)
            o_ref[qr, cs] = (o / l).astype(BF16)


def _na_start(r, rows):
    return jnp.clip(r - NA_KH // 2, 0, rows - NA_KH)


def _na_call(qkv, bias, batch, rows):
    n = qkv.shape[0]
    assert rows >= NA_WIN and rows % NA_ROWS == 0
    steps = rows // NA_ROWS

    def win_tok(b, j):
        w0 = jnp.minimum(_na_start(j * NA_ROWS, rows), rows - NA_WIN)
        return (b * rows + w0) * GRID_W

    def bias_spec(i):
        return pl.BlockSpec((None, NA_HEADS, GRID_W, NA_KH * GRID_W),
                            lambda b, j: (j * NA_ROWS + i - _na_start(j * NA_ROWS + i, rows), 0, 0, 0))

    win = (pl.Element(NA_WIN * GRID_W), pl.Element(NA_WIDTH))
    blk = (NA_ROWS * GRID_W, NA_WIDTH)
    return pl.pallas_call(
        functools.partial(_na_kernel, rows=rows), grid=(batch, steps),
        in_specs=[pl.BlockSpec(blk, lambda b, j: (b * steps + j, 0)),
                  pl.BlockSpec(win, lambda b, j: (win_tok(b, j), NA_WIDTH)),
                  pl.BlockSpec(win, lambda b, j: (win_tok(b, j), 2 * NA_WIDTH))]
                 + [bias_spec(i) for i in range(NA_ROWS)],
        out_specs=pl.BlockSpec(blk, lambda b, j: (b * steps + j, 0)),
        out_shape=jax.ShapeDtypeStruct((n, NA_WIDTH), BF16),
        scratch_shapes=[pltpu.VMEM((NA_ROWS * NA_HEADS, GRID_W, NA_KH * GRID_W), F32)],
        compiler_params=_cparams("parallel", "arbitrary"), name="na_attn")(
            qkv, qkv, qkv, *([bias] * NA_ROWS))


def _na_bias_tables(rpb):
    c = np.arange(GRID_W)
    c_start = np.clip(c - NA_KW // 2, 0, GRID_W - NA_KW)
    col_mask = (c[None, :] >= c_start[:, None]) & (c[None, :] < c_start[:, None] + NA_KW)
    dc = np.clip(c[None, :] - c[:, None], -(NA_KW - 1), NA_KW - 1) + (NA_KW - 1)
    onehot = jnp.asarray(dc[:, :, None] == np.arange(2 * NA_KW - 1), F32)
    t = jnp.einsum("lhab,qkb->lhqak", rpb.astype(F32), onehot, precision=lax.Precision.HIGHEST)
    t = jnp.where(col_mask[None, None, :, None, :], t, NEG_BIG)
    per_cls = [t[:, :, :, NA_KH - 1 - cls:2 * NA_KH - 1 - cls] for cls in range(NA_KH)]
    t = jnp.stack(per_cls, axis=1)
    return t.reshape(rpb.shape[0], NA_KH, NA_HEADS, GRID_W, NA_KH * GRID_W)


def _hg_constants(reverse):
    n = HG_CHUNK
    t = np.arange(n)
    if reverse:
        tri = (t[None, :] >= t[:, None])
    else:
        tri = (t[None, :] <= t[:, None])
    masks = []
    for c in HG_LEVELS:
        blk = t // (2 * c)
        upper = (t % (2 * c)) >= c
        same = blk[:, None] == blk[None, :]
        if reverse:
            m = same & (~upper)[:, None] & upper[None, :]
        else:
            m = same & upper[:, None] & (~upper)[None, :]
        masks.append(m)
    masks.append(np.eye(n, dtype=bool))
    tri = np.kron(np.eye(HG_GROUP, dtype=bool), tri)
    return jnp.asarray(tri, BF16), jnp.asarray(np.stack(masks), F32)


def _hgrn_kernel(*refs, reverse, final):
    if final:
        q_ref, g_ref, k_ref, v_ref, tri_ref, lm_ref, gate_ref, prev_ref, ng_ref, o_ref, st_ref = refs
    else:
        q_ref, g_ref, k_ref, v_ref, tri_ref, lm_ref, o_ref, st_ref = refs
    n = HG_CHUNK
    grp = HG_GROUP
    w = HG_WIDTH
    m = grp * n

    @pl.when(pl.program_id(1) == 0)
    def _():
        st_ref[...] = jnp.zeros_like(st_ref)

    g = g_ref[...]
    kk = k_ref[...].astype(F32)
    qq = q_ref[...]
    v_b = v_ref[...]

    g1 = g.astype(BF16)
    r1 = g - g1.astype(F32)
    g2 = r1.astype(BF16)
    g3 = (r1 - g2.astype(F32)).astype(BF16)
    tri = tri_ref[...]
    b = _dot(tri, g1) + _dot(tri, g2) + _dot(tri, g3)
    last = 0 if reverse else n - 1
    bc = b.reshape(grp, n, w)
    b_last = bc[:, last:last + 1, :]
    e_last = jnp.exp(b_last)

    row = lax.broadcasted_iota(jnp.int32, (m, 1), 0)
    qs, ks = [], []
    for c in HG_LEVELS:
        ref_row = c if reverse else c - 1
        if c == 1:
            d = jnp.where((row % 2) == (0 if reverse else 1), g, 0.0)
        elif c == 2:
            b3 = b.reshape(m // 8, 8, w)
            sub = lax.broadcasted_iota(jnp.int32, (1, 8, 1), 1)
            r_lo = jnp.broadcast_to(b3[:, ref_row:ref_row + 1, :], b3.shape)
            r_hi = jnp.broadcast_to(b3[:, ref_row + 4:ref_row + 5, :], b3.shape)
            d = -jnp.abs(b3 - jnp.where(sub < 4, r_lo, r_hi)).reshape(m, w)
        else:
            b3 = b.reshape(m // (2 * c), 2 * c, w)
            d = -jnp.abs(b3 - jnp.broadcast_to(b3[:, ref_row:ref_row + 1, :], b3.shape)).reshape(m, w)
        e = jnp.exp(d)
        qs.append((qq * e).astype(BF16))
        ks.append((kk * e).astype(BF16))
    qs.append(qq.astype(BF16))
    ks.append(kk.astype(BF16))
    q_in = (qq * jnp.exp(b)).astype(BF16)
    k_out = (kk * jnp.exp(b_last - bc).reshape(m, w)).astype(BF16)

    intra = {}
    for ci in range(grp):
        rs = slice(ci * n, (ci + 1) * n)
        for h in range(HG_HEADS):
            hs = slice(h * HG_HEAD_DIM, (h + 1) * HG_HEAD_DIM)
            a = None
            for li in range(len(qs)):
                term = lm_ref[li] * _dot_nt(qs[li][rs, hs], ks[li][rs, hs])
                a = term if a is None else a + term
            intra[ci, h] = _dot(a.astype(BF16), v_b[rs, hs])

    if final:
        gs = gate_ref[...].astype(F32)
    for ci in (range(grp - 1, -1, -1) if reverse else range(grp)):
        rs = slice(ci * n, (ci + 1) * n)
        for h in range(HG_HEADS):
            hs = slice(h * HG_HEAD_DIM, (h + 1) * HG_HEAD_DIM)
            st = st_ref[h]
            o = intra[ci, h] + _dot_nt(q_in[rs, hs], st.astype(BF16))
            st_ref[h] = st * e_last[ci][:, hs] + _dot_tn(v_b[rs, hs], k_out[rs, hs])
            if final:
                o = o + prev_ref[rs, hs]
                o = o * lax.rsqrt(jnp.mean(o * o, axis=-1, keepdims=True) + RMS_EPS) * ng_ref[:, hs]
                o_ref[rs, hs] = (o * gs[rs, hs]).astype(BF16)
            else:
                o_ref[rs, hs] = o


def _hgrn_call(hf, hk, batch, seq, reverse, prev=None, norm_g=None):
    n = hf.shape[0]
    rows = HG_GROUP * HG_CHUNK
    assert seq % rows == 0
    nc = seq // rows
    final = prev is not None
    tri, masks = _hg_constants(reverse)

    def tok(b, c):
        return b * nc + ((nc - 1 - c) if reverse else c)

    def col(j):
        return pl.BlockSpec((rows, HG_WIDTH), lambda b, c: (tok(b, c), j))

    vec = pl.BlockSpec((1, HG_WIDTH), lambda b, c: (0, 0))
    d = 1 if reverse else 0
    in_specs = [col(0), col(1 + d), col(d), col(2),
                pl.BlockSpec((rows, rows), lambda b, c: (0, 0)),
                pl.BlockSpec(masks.shape, lambda b, c: (0, 0, 0))]
    args = [hf, hf, hk, hk, tri, masks]
    if final:
        in_specs += [col(3), col(0), vec]
        args += [hk, prev, norm_g.reshape(1, HG_WIDTH)]
    return pl.pallas_call(
        functools.partial(_hgrn_kernel, reverse=reverse, final=final),
        grid=(batch, nc), in_specs=in_specs,
        out_specs=pl.BlockSpec((rows, HG_WIDTH), lambda b, c: (tok(b, c), 0)),
        out_shape=jax.ShapeDtypeStruct((n, HG_WIDTH), BF16 if final else F32),
        scratch_shapes=[pltpu.VMEM((HG_HEADS, HG_HEAD_DIM, HG_HEAD_DIM), F32)],
        compiler_params=_cparams("arbitrary", "arbitrary"),
        name="hgrn_bwd" if reverse else "hgrn_fwd")(*args)


def _mix_kernel(na_ref, hg_ref, gt_ref, x_ref, wna_ref, whg_ref, wout_ref, g_ref, b_ref, wr_ref,
                x1_ref, x1b_ref, aff_ref):
    hm = x_ref.shape[0] // MIX_SPLIT
    for s in range(MIX_SPLIT):
        rs = slice(s * hm, (s + 1) * hm)
        a = _dot(na_ref[rs, :], wna_ref[...])
        r = _dot(hg_ref[rs, :], whg_ref[...])
        mix = gt_ref[rs, :D_MODEL].astype(F32) * a + gt_ref[rs, D_MODEL:].astype(F32) * r
        y = _dot(mix.astype(BF16), wout_ref[...])
        x1 = _layer_norm(ALPHA * x_ref[rs, :] + y, g_ref[...], b_ref[...])
        x1_ref[rs, :] = x1
        x_hi = x1.astype(BF16)
        x1b_ref[rs, :] = x_hi
        x_lo = (x1 - x_hi.astype(F32)).astype(BF16)
        p = _dot(x_hi, wr_ref[...]) + _dot(x_lo, wr_ref[...])
        logits = p[:, :N_EXPERTS] + p[:, N_EXPERTS:]
        e = jnp.exp(logits - jnp.max(logits, axis=-1, keepdims=True))
        aff_ref[rs, :] = e / jnp.sum(e, axis=-1, keepdims=True)


def _mix_call(na, hg, gt, x, wna, whg, wout, g, b, wr2, tm=512):
    n, d = x.shape
    const = lambda shape: pl.BlockSpec(shape, lambda i: (0,) * len(shape))
    return pl.pallas_call(
        _mix_kernel, grid=(n // tm,),
        in_specs=[pl.BlockSpec((tm, NA_WIDTH), lambda i: (i, 0)),
                  pl.BlockSpec((tm, HG_WIDTH), lambda i: (i, 0)),
                  pl.BlockSpec((tm, 2 * D_MODEL), lambda i: (i, 0)),
                  pl.BlockSpec((tm, d), lambda i: (i, 0)),
                  const((NA_WIDTH, d)), const((HG_WIDTH, d)), const((d, d)),
                  const((1, d)), const((1, d)), const((d, 2 * N_EXPERTS))],
        out_specs=[pl.BlockSpec((tm, d), lambda i: (i, 0)),
                   pl.BlockSpec((tm, d), lambda i: (i, 0)),
                   pl.BlockSpec((tm, N_EXPERTS), lambda i: (i, 0))],
        out_shape=[jax.ShapeDtypeStruct((n, d), F32), jax.ShapeDtypeStruct((n, d), BF16),
                   jax.ShapeDtypeStruct((n, N_EXPERTS), F32)],
        compiler_params=_cparams("parallel"), name="mix_router")(
            na, hg, gt, x, wna, whg, wout, g.reshape(1, d), b.reshape(1, d), wr2)


def _ffn_kernel(xs_ref, wg_ref, wu_ref, wd_ref, gate_ref, o_ref, acc_ref):
    k = pl.program_id(2)
    xs = xs_ref[...]
    hg = _dot(xs, wg_ref[...])
    hu = _dot(xs, wu_ref[...])
    part = _dot((hg * _sigmoid(hg) * hu).astype(BF16), wd_ref[...])

    @pl.when(k == 0)
    def _():
        acc_ref[...] = part

    @pl.when(k > 0)
    def _():
        acc_ref[...] += part

    @pl.when(k == pl.num_programs(2) - 1)
    def _():
        o_ref[...] = (acc_ref[...] * gate_ref[...]).astype(BF16)


def _ffn_call(xs, wg, wu, wd, layer, gate, tc=1024, tf=1024):
    e, c, d = xs.shape
    f = wg.shape[-1]
    tc = min(tc, c)
    return pl.pallas_call(
        _ffn_kernel, grid=(e, c // tc, f // tf),
        in_specs=[pl.BlockSpec((None, tc, d), lambda i, j, k: (i, j, 0)),
                  pl.BlockSpec((None, None, d, tf), lambda i, j, k: (layer, i, 0, k)),
                  pl.BlockSpec((None, None, d, tf), lambda i, j, k: (layer, i, 0, k)),
                  pl.BlockSpec((None, None, tf, d), lambda i, j, k: (layer, i, k, 0)),
                  pl.BlockSpec((None, tc, 1), lambda i, j, k: (i, j, 0))],
        out_specs=pl.BlockSpec((None, tc, d), lambda i, j, k: (i, j, 0)),
        out_shape=jax.ShapeDtypeStruct((e, c, d), BF16),
        scratch_shapes=[pltpu.VMEM((tc, d), F32)],
        compiler_params=_cparams("parallel", "parallel", "arbitrary"), name="expert_ffn")(
            xs, wg, wu, wd, gate.reshape(e, c, 1))


def _rank_kernel(aff_ref, tau_ref, need_ref, lrank_ref, base_ref, ceq_ref, csel_ref):
    @pl.when(pl.program_id(0) == 0)
    def _():
        ceq_ref[...] = jnp.zeros_like(ceq_ref)
        csel_ref[...] = jnp.zeros_like(csel_ref)

    tau = tau_ref[:, 0:1]
    u = lax.broadcasted_iota(jnp.int32, (RANK_TILE, RANK_TILE), 0)
    t = lax.broadcasted_iota(jnp.int32, (RANK_TILE, RANK_TILE), 1)
    before = jnp.where(u < t, 1.0, 0.0).astype(BF16)
    ones = jnp.ones((RANK_TILE, 128), BF16)
    for i in range(RANK_STEP):
        ts = slice(i * RANK_TILE, (i + 1) * RANK_TILE)
        a = aff_ref[:, ts]
        eq = a == tau
        eq_f = jnp.where(eq, 1.0, 0.0).astype(BF16)
        eq_before = ceq_ref[:, 0:1] + _dot(eq_f, before)
        sel = jnp.logical_or(a > tau, jnp.logical_and(eq, eq_before < need_ref[:, 0:1]))
        sel_f = jnp.where(sel, 1.0, 0.0).astype(BF16)
        lrank_ref[:, ts] = jnp.where(sel, _dot(sel_f, before), -1.0).astype(jnp.int32)
        base_ref[i] = csel_ref[...].astype(jnp.int32)
        ceq_ref[...] += _dot(eq_f, ones)
        csel_ref[...] += _dot(sel_f, ones)


def _rank_call(aff_t, tau, need):
    e, n = aff_t.shape
    nt = n // RANK_TILE
    step = RANK_STEP * RANK_TILE
    assert n % step == 0
    rep = lambda v: jnp.broadcast_to(v.astype(F32).reshape(e, 1), (e, 128))
    vec = pl.BlockSpec((e, 128), lambda j: (0, 0))
    return pl.pallas_call(
        _rank_kernel, grid=(n // step,),
        in_specs=[pl.BlockSpec((e, step), lambda j: (0, j)), vec, vec],
        out_specs=[pl.BlockSpec((e, step), lambda j: (0, j)),
                   pl.BlockSpec((RANK_STEP, e, 128), lambda j: (j, 0, 0))],
        out_shape=[jax.ShapeDtypeStruct((e, n), jnp.int32),
                   jax.ShapeDtypeStruct((nt, e, 128), jnp.int32)],
        scratch_shapes=[pltpu.VMEM((e, 128), F32), pltpu.VMEM((e, 128), F32)],
        compiler_params=_cparams("arbitrary"), name="slot_rank")(aff_t, rep(tau), rep(need))


def _combine_kernel(base_ref, npass_ref, x_ref, lrank_ref, g_ref, b_ref, out_hbm, o_ref,
                    buf_ref, acc_ref, sem_ref, *, cap):
    j = pl.program_id(0)
    nt = pl.num_programs(0)

    def win_start(tile, e, p):
        r = base_ref[e * nt + tile] + p * SLOTS_PER_PASS
        return jnp.minimum((r // 16) * 16, cap - WIN)

    def window_copy(tile, e, p, slot):
        a = pl.multiple_of(win_start(tile, e, p), 16)
        return pltpu.make_async_copy(out_hbm.at[e, pl.ds(a, WIN), :],
                                     buf_ref.at[slot, pl.ds(e * WIN, WIN), :], sem_ref.at[slot])

    def start_all(tile, p, slot):
        for e in range(N_EXPERTS):
            window_copy(tile, e, p, slot).start()

    def wait_all(tile, p, slot):
        for e in range(N_EXPERTS):
            window_copy(tile, e, p, slot).wait()

    def add_pass(p, slot):
        row = lax.broadcasted_iota(jnp.int32, (WIN, RANK_TILE), 0)
        sel = []
        for e in range(N_EXPERTS):
            lr = lrank_ref[e:e + 1, :]
            off = base_ref[e * nt + j] - win_start(j, e, p)
            pos = jnp.where(jnp.logical_and(lr >= p * SLOTS_PER_PASS, lr < (p + 1) * SLOTS_PER_PASS),
                            lr + off, -1)
            sel.append(jnp.where(row == pos, 1.0, 0.0).astype(BF16))
        acc_ref[...] += _dot_tn(jnp.concatenate(sel, axis=0), buf_ref[slot])

    slot = j % 2

    @pl.when(j == 0)
    def _():
        start_all(0, 0, 0)

    @pl.when(j + 1 < nt)
    def _():
        start_all(j + 1, 0, 1 - slot)

    acc_ref[...] = jnp.zeros_like(acc_ref)
    wait_all(j, 0, slot)
    add_pass(0, slot)

    def extra(p, carry):
        start_all(j, p, slot)
        wait_all(j, p, slot)
        add_pass(p, slot)
        return carry

    lax.fori_loop(1, npass_ref[j], extra, 0)
    o_ref[...] = _layer_norm(ALPHA * x_ref[...] + acc_ref[...], g_ref[...], b_ref[...])


def _combine_call(x1, lrank, base, out, g, b):
    n, d = x1.shape
    e, cap, _ = out.shape
    nt = n // RANK_TILE
    assert cap >= WIN and cap % 16 == 0
    cnt = jnp.concatenate([base[:, 1:], jnp.full((e, 1), cap, jnp.int32)], axis=1) - base
    npass = jnp.maximum(1, (jnp.max(cnt, axis=0) + SLOTS_PER_PASS - 1) // SLOTS_PER_PASS).astype(jnp.int32)
    vec = pl.BlockSpec((1, d), lambda j, *_: (0, 0))
    grid_spec = pltpu.PrefetchScalarGridSpec(
        num_scalar_prefetch=2, grid=(nt,),
        in_specs=[pl.BlockSpec((RANK_TILE, d), lambda j, *_: (j, 0)),
                  pl.BlockSpec((e, RANK_TILE), lambda j, *_: (0, j)),
                  vec, vec, pl.BlockSpec(memory_space=pl.ANY)],
        out_specs=pl.BlockSpec((RANK_TILE, d), lambda j, *_: (j, 0)),
        scratch_shapes=[pltpu.VMEM((2, e * WIN, d), BF16), pltpu.VMEM((RANK_TILE, d), F32),
                        pltpu.SemaphoreType.DMA((2,))])
    return pl.pallas_call(
        functools.partial(_combine_kernel, cap=cap), grid_spec=grid_spec,
        out_shape=jax.ShapeDtypeStruct((n, d), F32),
        compiler_params=_cparams("arbitrary"), name="combine_ln")(
            base.reshape(-1), npass, x1, lrank, g.reshape(1, d), b.reshape(1, d), out)


def _hi_lo(w):
    hi = w.astype(BF16)
    lo = (w - hi.astype(F32)).astype(BF16)
    return jnp.concatenate([hi, lo], axis=-1)


def _lower_bounds(p):
    sm = jax.nn.softmax(p.astype(F32), axis=0)
    return jnp.cumsum(sm, axis=0) - sm[0:1]


def _trunk(x, prm):
    batch, seq, d = x.shape
    n = batch * seq
    rows = seq // GRID_W
    cap = EC_FACTOR * n // N_EXPERTS
    x = _ln_call(x.reshape(n, d), prm["ln_in_g"], prm["ln_in_b"])
    for l in range(DEPTH):
        qkv, gt, hf, hk = _inproj_call(x, prm["w_in"][l], prm["lb_fwd"][l], prm["lb_bwd"][l])
        na = _na_call(qkv, prm["na_bias"][l], batch, rows)
        o_f = _hgrn_call(hf, hk, batch, seq, reverse=False)
        hg = _hgrn_call(hf, hk, batch, seq, reverse=True, prev=o_f, norm_g=prm["hg_norm_g"][l])
        x1, x1b, aff = _mix_call(na, hg, gt, x, prm["w_branch_na"][l], prm["w_branch_hg"][l],
                                 prm["w_out"][l], prm["ln1_g"][l], prm["ln1_b"][l], prm["w_router2"][l])
        aff_t = aff.T
        gate, idx = lax.top_k(aff_t, cap)
        idx, gate = lax.sort((idx, gate), dimension=1, num_keys=1)
        tau = gate.min(axis=1)
        need = cap - jnp.sum(aff_t > tau[:, None], axis=1)
        lrank, base = _rank_call(aff_t, tau, need)
        xs = x1b[idx]
        out = _ffn_call(xs, prm["w_gate_e"], prm["w_up_e"], prm["w_down_e"], l, gate)
        x = _combine_call(x1, lrank, base[:, :, 0].T, out, prm["ln2_g"][l], prm["ln2_b"][l])
    return x.reshape(batch, seq, d)


@jax.jit
def kernel(x_prompt, x_sample, ln_in_g, ln_in_b, w_in, na_rpb, hg_lb_fwd, hg_lb_bwd, hg_norm_g,
           w_branch_na, w_branch_hg, w_out, ln1_g, ln1_b, w_router, w_gate_e, w_up_e, w_down_e,
           ln2_g, ln2_b):
    prm = dict(
        ln_in_g=ln_in_g, ln_in_b=ln_in_b, w_in=w_in.astype(BF16), na_bias=_na_bias_tables(na_rpb),
        lb_fwd=_lower_bounds(hg_lb_fwd), lb_bwd=_lower_bounds(hg_lb_bwd), hg_norm_g=hg_norm_g,
        w_branch_na=w_branch_na.astype(BF16), w_branch_hg=w_branch_hg.astype(BF16),
        w_out=w_out.astype(BF16), ln1_g=ln1_g, ln1_b=ln1_b,
        w_router2=_hi_lo(w_router),
        w_gate_e=w_gate_e.astype(BF16), w_up_e=w_up_e.astype(BF16), w_down_e=w_down_e.astype(BF16),
        ln2_g=ln2_g, ln2_b=ln2_b)
    return _trunk(x_prompt, prm), _trunk(x_sample, prm)
```

```python
import functools

import jax
import jax.numpy as jnp
import numpy as np
from jax import lax
from jax.experimental import pallas as pl
from jax.experimental.pallas import tpu as pltpu

F32 = jnp.float32
BF16 = jnp.bfloat16

D_MODEL = 1024
DEPTH = 4
GRID_W = 64
NA_HEADS = 8
NA_HEAD_DIM = 64
NA_WIDTH = NA_HEADS * NA_HEAD_DIM
NA_KH = 8
NA_KW = 16
NA_ROWS = 4
NA_WIN = NA_KH + NA_ROWS - 1
HG_HEADS = 4
HG_HEAD_DIM = 128
HG_WIDTH = HG_HEADS * HG_HEAD_DIM
HG_CHUNK = 64
N_EXPERTS = 16
EC_FACTOR = 2
D_FF = 2 * D_MODEL
IN_WIDTH = 3 * NA_WIDTH + 5 * HG_WIDTH + 2 * D_MODEL
ALPHA = (2 * DEPTH) ** 0.25
LN_EPS = 1e-5
RMS_EPS = 1e-6
NEG_BIG = -1e30
F_MIN = 1e-30

GATE_WIDTH = 2 * D_MODEL
HF_WIDTH = 3 * HG_WIDTH
HK_WIDTH = 4 * HG_WIDTH
MIX_SPLIT = 2
INPROJ_CHUNK = 256
HG_LEVELS = (32, 16, 8, 4, 2, 1)
HG_GROUP = 4
VMEM_LIMIT = 48 * 1024 * 1024
RANK_TILE = 256
RANK_STEP = 4
SLOTS_PER_PASS = 48
WIN = SLOTS_PER_PASS + 16
TAU_CHUNK = 2048
FILL_SPAN = 256


def _cparams(*sem):
    return pltpu.CompilerParams(dimension_semantics=sem, vmem_limit_bytes=VMEM_LIMIT)


def _sigmoid(x):
    return 0.5 * jnp.tanh(0.5 * x) + 0.5


def _layer_norm(x, g, b):
    mu = jnp.mean(x, axis=-1, keepdims=True)
    xc = x - mu
    var = jnp.mean(xc * xc, axis=-1, keepdims=True)
    return xc * lax.rsqrt(var + LN_EPS) * g + b


def _dot(a, b):
    return jnp.dot(a, b, preferred_element_type=F32)


def _dot_nt(a, b):
    return lax.dot_general(a, b, (((1,), (1,)), ((), ())), preferred_element_type=F32)


def _dot_tn(a, b):
    return lax.dot_general(a, b, (((0,), (0,)), ((), ())), preferred_element_type=F32)


def _ln_kernel(x_ref, g_ref, b_ref, o_ref):
    o_ref[...] = _layer_norm(x_ref[...], g_ref[...], b_ref[...])


def _ln_call(x, g, b, tm=512):
    n, d = x.shape
    row = pl.BlockSpec((tm, d), lambda i: (i, 0))
    vec = pl.BlockSpec((1, d), lambda i: (0, 0))
    return pl.pallas_call(
        _ln_kernel, grid=(n // tm,), in_specs=[row, vec, vec], out_specs=row,
        out_shape=jax.ShapeDtypeStruct((n, d), F32), compiler_params=_cparams("parallel"),
        name="ln_in")(x, g.reshape(1, d), b.reshape(1, d))


def _inproj_kernel(x_ref, w_ref, lbf_ref, lbb_ref, qkv_ref, gt_ref, hf_ref, hk_ref):
    xb = x_ref[...].astype(BF16)
    cw = INPROJ_CHUNK
    for c in range(IN_WIDTH // cw):
        acc = _dot(xb, w_ref[:, c * cw:(c + 1) * cw])
        o = c * cw
        if o < 3 * NA_WIDTH:
            if o < NA_WIDTH:
                acc = acc * (NA_HEAD_DIM ** -0.5)
            qkv_ref[:, o:o + cw] = acc.astype(BF16)
        elif o < 3 * NA_WIDTH + 5 * HG_WIDTH:
            o -= 3 * NA_WIDTH
            s, o = o // HG_WIDTH, o % HG_WIDTH
            if s == 0:
                hf_ref[:, o:o + cw] = acc * _sigmoid(acc)
            elif s in (1, 2):
                lb = (lbf_ref if s == 1 else lbb_ref)[:, o:o + cw]
                sg = _sigmoid(acc)
                f = lb + (1.0 - lb) * sg
                hf_ref[:, s * HG_WIDTH + o:s * HG_WIDTH + o + cw] = jnp.log(jnp.maximum(f, F_MIN))
                hk_ref[:, (s - 1) * HG_WIDTH + o:(s - 1) * HG_WIDTH + o + cw] = (
                    (1.0 - lb) * (1.0 - sg)).astype(BF16)
            elif s == 3:
                hk_ref[:, 2 * HG_WIDTH + o:2 * HG_WIDTH + o + cw] = acc.astype(BF16)
            else:
                hk_ref[:, 3 * HG_WIDTH + o:3 * HG_WIDTH + o + cw] = (acc * _sigmoid(acc)).astype(BF16)
        else:
            o -= 3 * NA_WIDTH + 5 * HG_WIDTH
            gt_ref[:, o:o + cw] = _sigmoid(acc).astype(BF16)


def _inproj_call(x, w, lb_fwd, lb_bwd, tm=256):
    n, d = x.shape
    vec = pl.BlockSpec((1, HG_WIDTH), lambda i: (0, 0))
    return pl.pallas_call(
        _inproj_kernel, grid=(n // tm,),
        in_specs=[pl.BlockSpec((tm, d), lambda i: (i, 0)),
                  pl.BlockSpec((d, IN_WIDTH), lambda i: (0, 0), pipeline_mode=pl.Buffered(1)), vec, vec],
        out_specs=[pl.BlockSpec((tm, 3 * NA_WIDTH), lambda i: (i, 0)),
                   pl.BlockSpec((tm, GATE_WIDTH), lambda i: (i, 0)),
                   pl.BlockSpec((tm, HF_WIDTH), lambda i: (i, 0)),
                   pl.BlockSpec((tm, HK_WIDTH), lambda i: (i, 0))],
        out_shape=[jax.ShapeDtypeStruct((n, 3 * NA_WIDTH), BF16),
                   jax.ShapeDtypeStruct((n, GATE_WIDTH), BF16),
                   jax.ShapeDtypeStruct((n, HF_WIDTH), F32),
                   jax.ShapeDtypeStruct((n, HK_WIDTH), BF16)],
        compiler_params=_cparams("parallel"), name="in_proj")(
            x, w, lb_fwd.reshape(1, HG_WIDTH), lb_bwd.reshape(1, HG_WIDTH))


def _na_kernel(*refs, rows):
    q_ref, k_ref, v_ref = refs[:3]
    bias_refs = refs[3:3 + NA_ROWS]
    o_ref, s_ref = refs[3 + NA_ROWS:]
    r0 = pl.program_id(1) * NA_ROWS
    w0 = jnp.minimum(_na_start(r0, rows), rows - NA_WIN)
    lane = lax.broadcasted_iota(jnp.int32, (1, 128), 1)
    lo = lane < NA_HEAD_DIM
    hi = jnp.logical_not(lo)
    offs = [pl.multiple_of((_na_start(r0 + i, rows) - w0) * GRID_W, GRID_W) for i in range(NA_ROWS)]
    for i in range(NA_ROWS):
        qr = slice(i * GRID_W, (i + 1) * GRID_W)
        kr = pl.ds(offs[i], NA_KH * GRID_W)
        for p in range(NA_HEADS // 2):
            cs = slice(p * 128, (p + 1) * 128)
            q2 = q_ref[qr, cs]
            k2 = k_ref[kr, cs]
            for hh in range(2):
                qm = jnp.where(hi if hh else lo, q2, jnp.zeros_like(q2))
                s = _dot_nt(qm, k2) + bias_refs[i][2 * p + hh]
                s_ref[i * NA_HEADS + 2 * p + hh] = s - jnp.max(s, axis=-1, keepdims=True)
    for i in range(NA_ROWS):
        qr = slice(i * GRID_W, (i + 1) * GRID_W)
        kr = pl.ds(offs[i], NA_KH * GRID_W)
        for p in range(NA_HEADS // 2):
            cs = slice(p * 128, (p + 1) * 128)
            v2 = v_ref[kr, cs]
            res = []
            for hh in range(2):
                e = jnp.exp(s_ref[i * NA_HEADS + 2 * p + hh]).astype(BF16)
                va = jnp.where(hi if hh else lo, v2, jnp.ones_like(v2))
                res.append(_dot(e, va))
            o = jnp.where(lo, res[0], res[1])
            l = pltpu.roll(jnp.where(lo, res[1], res[0]), NA_HEAD_DIM, 1)
            o_ref[qr, cs] = (o / l).astype(BF16)


def _na_start(r, rows):
    return jnp.clip(r - NA_KH // 2, 0, rows - NA_KH)


def _na_call(qkv, bias, batch, rows):
    n = qkv.shape[0]
    assert rows >= NA_WIN and rows % NA_ROWS == 0
    steps = rows // NA_ROWS

    def win_tok(b, j):
        w0 = jnp.minimum(_na_start(j * NA_ROWS, rows), rows - NA_WIN)
        return (b * rows + w0) * GRID_W

    def bias_spec(i):
        return pl.BlockSpec((None, NA_HEADS, GRID_W, NA_KH * GRID_W),
                            lambda b, j: (j * NA_ROWS + i - _na_start(j * NA_ROWS + i, rows), 0, 0, 0))

    win = (pl.Element(NA_WIN * GRID_W), pl.Element(NA_WIDTH))
    blk = (NA_ROWS * GRID_W, NA_WIDTH)
    return pl.pallas_call(
        functools.partial(_na_kernel, rows=rows), grid=(batch, steps),
        in_specs=[pl.BlockSpec(blk, lambda b, j: (b * steps + j, 0)),
                  pl.BlockSpec(win, lambda b, j: (win_tok(b, j), NA_WIDTH)),
                  pl.BlockSpec(win, lambda b, j: (win_tok(b, j), 2 * NA_WIDTH))]
                 + [bias_spec(i) for i in range(NA_ROWS)],
        out_specs=pl.BlockSpec(blk, lambda b, j: (b * steps + j, 0)),
        out_shape=jax.ShapeDtypeStruct((n, NA_WIDTH), BF16),
        scratch_shapes=[pltpu.VMEM((NA_ROWS * NA_HEADS, GRID_W, NA_KH * GRID_W), F32)],
        compiler_params=_cparams("parallel", "arbitrary"), name="na_attn")(
            qkv, qkv, qkv, *([bias] * NA_ROWS))


def _na_bias_tables(rpb):
    c = np.arange(GRID_W)
    c_start = np.clip(c - NA_KW // 2, 0, GRID_W - NA_KW)
    col_mask = (c[None, :] >= c_start[:, None]) & (c[None, :] < c_start[:, None] + NA_KW)
    dc = np.clip(c[None, :] - c[:, None], -(NA_KW - 1), NA_KW - 1) + (NA_KW - 1)
    onehot = jnp.asarray(dc[:, :, None] == np.arange(2 * NA_KW - 1), F32)
    t = jnp.einsum("lhab,qkb->lhqak", rpb.astype(F32), onehot, precision=lax.Precision.HIGHEST)
    t = jnp.where(col_mask[None, None, :, None, :], t, NEG_BIG)
    per_cls = [t[:, :, :, NA_KH - 1 - cls:2 * NA_KH - 1 - cls] for cls in range(NA_KH)]
    t = jnp.stack(per_cls, axis=1)
    return t.reshape(rpb.shape[0], NA_KH, NA_HEADS, GRID_W, NA_KH * GRID_W)


def _hg_constants(reverse):
    n = HG_CHUNK
    t = np.arange(n)
    if reverse:
        tri = (t[None, :] >= t[:, None])
    else:
        tri = (t[None, :] <= t[:, None])
    masks = []
    for c in HG_LEVELS:
        blk = t // (2 * c)
        upper = (t % (2 * c)) >= c
        same = blk[:, None] == blk[None, :]
        if reverse:
            m = same & (~upper)[:, None] & upper[None, :]
        else:
            m = same & upper[:, None] & (~upper)[None, :]
        masks.append(m)
    masks.append(np.eye(n, dtype=bool))
    tri = np.kron(np.eye(HG_GROUP, dtype=bool), tri)
    return jnp.asarray(tri, BF16), jnp.asarray(np.stack(masks), F32)


def _hgrn_kernel(*refs, reverse, final):
    if final:
        q_ref, g_ref, k_ref, v_ref, tri_ref, lm_ref, gate_ref, prev_ref, ng_ref, o_ref, st_ref = refs
    else:
        q_ref, g_ref, k_ref, v_ref, tri_ref, lm_ref, o_ref, st_ref = refs
    n = HG_CHUNK
    grp = HG_GROUP
    w = HG_WIDTH
    m = grp * n

    @pl.when(pl.program_id(1) == 0)
    def _():
        st_ref[...] = jnp.zeros_like(st_ref)

    g = g_ref[...]
    kk = k_ref[...].astype(F32)
    qq = q_ref[...]
    v_b = v_ref[...]

    g1 = g.astype(BF16)
    r1 = g - g1.astype(F32)
    g2 = r1.astype(BF16)
    g3 = (r1 - g2.astype(F32)).astype(BF16)
    tri = tri_ref[...]
    b = _dot(tri, g1) + _dot(tri, g2) + _dot(tri, g3)
    last = 0 if reverse else n - 1
    bc = b.reshape(grp, n, w)
    b_last = bc[:, last:last + 1, :]
    e_last = jnp.exp(b_last)

    row = lax.broadcasted_iota(jnp.int32, (m, 1), 0)
    qs, ks = [], []
    for c in HG_LEVELS:
        ref_row = c if reverse else c - 1
        if c == 1:
            d = jnp.where((row % 2) == (0 if reverse else 1), g, 0.0)
        elif c == 2:
            b3 = b.reshape(m // 8, 8, w)
            sub = lax.broadcasted_iota(jnp.int32, (1, 8, 1), 1)
            r_lo = jnp.broadcast_to(b3[:, ref_row:ref_row + 1, :], b3.shape)
            r_hi = jnp.broadcast_to(b3[:, ref_row + 4:ref_row + 5, :], b3.shape)
            d = -jnp.abs(b3 - jnp.where(sub < 4, r_lo, r_hi)).reshape(m, w)
        else:
            b3 = b.reshape(m // (2 * c), 2 * c, w)
            d = -jnp.abs(b3 - jnp.broadcast_to(b3[:, ref_row:ref_row + 1, :], b3.shape)).reshape(m, w)
        e = jnp.exp(d)
        qs.append((qq * e).astype(BF16))
        ks.append((kk * e).astype(BF16))
    qs.append(qq.astype(BF16))
    ks.append(kk.astype(BF16))
    q_in = (qq * jnp.exp(b)).astype(BF16)
    k_out = (kk * jnp.exp(b_last - bc).reshape(m, w)).astype(BF16)

    intra = {}
    for ci in range(grp):
        rs = slice(ci * n, (ci + 1) * n)
        for h in range(HG_HEADS):
            hs = slice(h * HG_HEAD_DIM, (h + 1) * HG_HEAD_DIM)
            a = None
            for li in range(len(qs)):
                term = lm_ref[li] * _dot_nt(qs[li][rs, hs], ks[li][rs, hs])
                a = term if a is None else a + term
            intra[ci, h] = _dot(a.astype(BF16), v_b[rs, hs])

    if final:
        gs = gate_ref[...].astype(F32)
    for ci in (range(grp - 1, -1, -1) if reverse else range(grp)):
        rs = slice(ci * n, (ci + 1) * n)
        for h in range(HG_HEADS):
            hs = slice(h * HG_HEAD_DIM, (h + 1) * HG_HEAD_DIM)
            st = st_ref[h]
            o = intra[ci, h] + _dot_nt(q_in[rs, hs], st.astype(BF16))
            st_ref[h] = st * e_last[ci][:, hs] + _dot_tn(v_b[rs, hs], k_out[rs, hs])
            if final:
                o = o + prev_ref[rs, hs]
                o = o * lax.rsqrt(jnp.mean(o * o, axis=-1, keepdims=True) + RMS_EPS) * ng_ref[:, hs]
                o_ref[rs, hs] = (o * gs[rs, hs]).astype(BF16)
            else:
                o_ref[rs, hs] = o


def _hgrn_call(hf, hk, batch, seq, reverse, prev=None, norm_g=None):
    n = hf.shape[0]
    rows = HG_GROUP * HG_CHUNK
    assert seq % rows == 0
    nc = seq // rows
    final = prev is not None
    tri, masks = _hg_constants(reverse)

    def tok(b, c):
        return b * nc + ((nc - 1 - c) if reverse else c)

    def col(j):
        return pl.BlockSpec((rows, HG_WIDTH), lambda b, c: (tok(b, c), j))

    vec = pl.BlockSpec((1, HG_WIDTH), lambda b, c: (0, 0))
    d = 1 if reverse else 0
    in_specs = [col(0), col(1 + d), col(d), col(2),
                pl.BlockSpec((rows, rows), lambda b, c: (0, 0)),
                pl.BlockSpec(masks.shape, lambda b, c: (0, 0, 0))]
    args = [hf, hf, hk, hk, tri, masks]
    if final:
        in_specs += [col(3), col(0), vec]
        args += [hk, prev, norm_g.reshape(1, HG_WIDTH)]
    return pl.pallas_call(
        functools.partial(_hgrn_kernel, reverse=reverse, final=final),
        grid=(batch, nc), in_specs=in_specs,
        out_specs=pl.BlockSpec((rows, HG_WIDTH), lambda b, c: (tok(b, c), 0)),
        out_shape=jax.ShapeDtypeStruct((n, HG_WIDTH), BF16 if final else F32),
        scratch_shapes=[pltpu.VMEM((HG_HEADS, HG_HEAD_DIM, HG_HEAD_DIM), F32)],
        compiler_params=_cparams("arbitrary", "arbitrary"),
        name="hgrn_bwd" if reverse else "hgrn_fwd")(*args)


def _mix_kernel(na_ref, hg_ref, gt_ref, x_ref, wna_ref, whg_ref, wout_ref, g_ref, b_ref, wr_ref,
                x1_ref, x1b_ref, aff_ref):
    hm = x_ref.shape[0] // MIX_SPLIT
    for s in range(MIX_SPLIT):
        rs = slice(s * hm, (s + 1) * hm)
        a = _dot(na_ref[rs, :], wna_ref[...])
        r = _dot(hg_ref[rs, :], whg_ref[...])
        mix = gt_ref[rs, :D_MODEL].astype(F32) * a + gt_ref[rs, D_MODEL:].astype(F32) * r
        y = _dot(mix.astype(BF16), wout_ref[...])
        x1 = _layer_norm(ALPHA * x_ref[rs, :] + y, g_ref[...], b_ref[...])
        x1_ref[rs, :] = x1
        x_hi = x1.astype(BF16)
        x1b_ref[rs, :] = x_hi
        x_lo = (x1 - x_hi.astype(F32)).astype(BF16)
        p = _dot(x_hi, wr_ref[...]) + _dot(x_lo, wr_ref[...])
        logits = p[:, :N_EXPERTS] + p[:, N_EXPERTS:]
        e = jnp.exp(logits - jnp.max(logits, axis=-1, keepdims=True))
        aff_ref[rs, :] = e / jnp.sum(e, axis=-1, keepdims=True)


def _mix_call(na, hg, gt, x, wna, whg, wout, g, b, wr2, tm=512):
    n, d = x.shape
    const = lambda shape: pl.BlockSpec(shape, lambda i: (0,) * len(shape))
    return pl.pallas_call(
        _mix_kernel, grid=(n // tm,),
        in_specs=[pl.BlockSpec((tm, NA_WIDTH), lambda i: (i, 0)),
                  pl.BlockSpec((tm, HG_WIDTH), lambda i: (i, 0)),
                  pl.BlockSpec((tm, 2 * D_MODEL), lambda i: (i, 0)),
                  pl.BlockSpec((tm, d), lambda i: (i, 0)),
                  const((NA_WIDTH, d)), const((HG_WIDTH, d)), const((d, d)),
                  const((1, d)), const((1, d)), const((d, 2 * N_EXPERTS))],
        out_specs=[pl.BlockSpec((tm, d), lambda i: (i, 0)),
                   pl.BlockSpec((tm, d), lambda i: (i, 0)),
                   pl.BlockSpec((tm, N_EXPERTS), lambda i: (i, 0))],
        out_shape=[jax.ShapeDtypeStruct((n, d), F32), jax.ShapeDtypeStruct((n, d), BF16),
                   jax.ShapeDtypeStruct((n, N_EXPERTS), F32)],
        compiler_params=_cparams("parallel"), name="mix_router")(
            na, hg, gt, x, wna, whg, wout, g.reshape(1, d), b.reshape(1, d), wr2)


def _ffn_kernel(xs_ref, wg_ref, wu_ref, wd_ref, gate_ref, o_ref, acc_ref):
    k = pl.program_id(2)
    xs = xs_ref[...]
    hg = _dot(xs, wg_ref[...])
    hu = _dot(xs, wu_ref[...])
    part = _dot((hg * _sigmoid(hg) * hu).astype(BF16), wd_ref[...])

    @pl.when(k == 0)
    def _():
        acc_ref[...] = part

    @pl.when(k > 0)
    def _():
        acc_ref[...] += part

    @pl.when(k == pl.num_programs(2) - 1)
    def _():
        o_ref[...] = (acc_ref[...] * gate_ref[...]).astype(BF16)


def _ffn_call(xs, wg, wu, wd, layer, gate, tc=1024, tf=1024):
    e, c, d = xs.shape
    f = wg.shape[-1]
    tc = min(tc, c)
    return pl.pallas_call(
        _ffn_kernel, grid=(e, c // tc, f // tf),
        in_specs=[pl.BlockSpec((None, tc, d), lambda i, j, k: (i, j, 0)),
                  pl.BlockSpec((None, None, d, tf), lambda i, j, k: (layer, i, 0, k)),
                  pl.BlockSpec((None, None, d, tf), lambda i, j, k: (layer, i, 0, k)),
                  pl.BlockSpec((None, None, tf, d), lambda i, j, k: (layer, i, k, 0)),
                  pl.BlockSpec((None, tc, 1), lambda i, j, k: (i, j, 0))],
        out_specs=pl.BlockSpec((None, tc, d), lambda i, j, k: (i, j, 0)),
        out_shape=jax.ShapeDtypeStruct((e, c, d), BF16),
        scratch_shapes=[pltpu.VMEM((tc, d), F32)],
        compiler_params=_cparams("parallel", "parallel", "arbitrary"), name="expert_ffn")(
            xs, wg, wu, wd, gate.reshape(e, c, 1))


def _rank_kernel(aff_ref, tau_ref, need_ref, lrank_ref, base_ref, ceq_ref, csel_ref):
    @pl.when(pl.program_id(0) == 0)
    def _():
        ceq_ref[...] = jnp.zeros_like(ceq_ref)
        csel_ref[...] = jnp.zeros_like(csel_ref)

    tau = tau_ref[:, 0:1]
    u = lax.broadcasted_iota(jnp.int32, (RANK_TILE, RANK_TILE), 0)
    t = lax.broadcasted_iota(jnp.int32, (RANK_TILE, RANK_TILE), 1)
    before = jnp.where(u < t, 1.0, 0.0).astype(BF16)
    ones = jnp.ones((RANK_TILE, 128), BF16)
    for i in range(RANK_STEP):
        ts = slice(i * RANK_TILE, (i + 1) * RANK_TILE)
        a = aff_ref[:, ts]
        eq = a == tau
        eq_f = jnp.where(eq, 1.0, 0.0).astype(BF16)
        eq_before = ceq_ref[:, 0:1] + _dot(eq_f, before)
        sel = jnp.logical_or(a > tau, jnp.logical_and(eq, eq_before < need_ref[:, 0:1]))
        sel_f = jnp.where(sel, 1.0, 0.0).astype(BF16)
        lrank_ref[:, ts] = jnp.where(sel, _dot(sel_f, before), -1.0).astype(jnp.int32)
        base_ref[i] = csel_ref[...].astype(jnp.int32)
        ceq_ref[...] += _dot(eq_f, ones)
        csel_ref[...] += _dot(sel_f, ones)


def _rank_call(aff_t, tau, need):
    e, n = aff_t.shape
    nt = n // RANK_TILE
    step = RANK_STEP * RANK_TILE
    assert n % step == 0
    vec = pl.BlockSpec((e, 128), lambda j: (0, 0))
    return pl.pallas_call(
        _rank_kernel, grid=(n // step,),
        in_specs=[pl.BlockSpec((e, step), lambda j: (0, j)), vec, vec],
        out_specs=[pl.BlockSpec((e, step), lambda j: (0, j)),
                   pl.BlockSpec((RANK_STEP, e, 128), lambda j: (j, 0, 0))],
        out_shape=[jax.ShapeDtypeStruct((e, n), jnp.int32),
                   jax.ShapeDtypeStruct((nt, e, 128), jnp.int32)],
        scratch_shapes=[pltpu.VMEM((e, 128), F32), pltpu.VMEM((e, 128), F32)],
        compiler_params=_cparams("arbitrary"), name="slot_rank")(aff_t, tau, need)


def _tau_kernel(aff_ref, tau_ref, need_ref, *, cap):
    e, n = aff_ref.shape

    def count_ge(cand):
        def body(i, acc):
            c0 = pl.multiple_of(i * TAU_CHUNK, TAU_CHUNK)
            bits = pltpu.bitcast(aff_ref[:, pl.ds(c0, TAU_CHUNK)], jnp.int32)
            hit = jnp.where(bits >= cand, 1.0, 0.0)
            for k in range(TAU_CHUNK // 128):
                acc = acc + hit[:, k * 128:(k + 1) * 128]
            return acc
        acc = lax.fori_loop(0, n // TAU_CHUNK, body, jnp.zeros((e, 128), F32))
        return jnp.sum(acc, axis=1, keepdims=True)

    lo = jnp.zeros((e, 1), jnp.int32)
    for bit in range(30, -1, -1):
        cand = lo | (1 << bit)
        lo = jnp.where(count_ge(cand) >= cap, cand, lo)
    tau_ref[...] = jnp.broadcast_to(pltpu.bitcast(lo, F32), (e, 128))
    need_ref[...] = jnp.broadcast_to(cap - count_ge(lo + 1), (e, 128))


def _tau_call(aff_t, cap):
    e, n = aff_t.shape
    assert n % TAU_CHUNK == 0
    full = pl.BlockSpec((e, n), lambda i: (0, 0))
    vec = pl.BlockSpec((e, 128), lambda i: (0, 0))
    return pl.pallas_call(
        functools.partial(_tau_kernel, cap=cap), grid=(1,), in_specs=[full], out_specs=[vec, vec],
        out_shape=[jax.ShapeDtypeStruct((e, 128), F32)] * 2,
        compiler_params=_cparams("arbitrary"), name="route_tau")(aff_t)


def _fill_kernel(base_ref, next_ref, npass_ref, lrank_ref, idx_ref, *, cap):
    j = pl.program_id(0)
    nt = pl.num_programs(0)

    @pl.when(j == 0)
    def _():
        idx_ref[...] = jnp.zeros_like(idx_ref)

    tok = j * RANK_TILE + lax.broadcasted_iota(jnp.int32, (8, RANK_TILE), 1)
    sub = lax.broadcasted_iota(jnp.int32, (8, RANK_TILE), 0)
    vals = jnp.where(sub == 0, tok // 256, jnp.where(sub == 1, tok % 256, 0)).astype(F32).astype(BF16)
    row = lax.broadcasted_iota(jnp.int32, (WIN, RANK_TILE), 0)
    lane = lax.broadcasted_iota(jnp.int32, (1, FILL_SPAN), 1)

    def one_pass(p, carry):
        starts, sel = [], []
        for e in range(N_EXPERTS):
            s0 = base_ref[e * nt + j] + p * SLOTS_PER_PASS
            a = jnp.minimum((s0 // 16) * 16, cap - WIN)
            starts.append((s0, a))
            lr = lrank_ref[e:e + 1, :]
            pos = jnp.where(jnp.logical_and(lr >= p * SLOTS_PER_PASS, lr < (p + 1) * SLOTS_PER_PASS),
                            lr - p * SLOTS_PER_PASS + (s0 - a), -1)
            sel.append(jnp.where(row == pos, 1.0, 0.0).astype(BF16))
        got = _dot_nt(vals, jnp.concatenate(sel, axis=0))
        ids = (got[0:1, :] * 256.0 + got[1:2, :]).astype(jnp.int32)
        for e in range(N_EXPERTS):
            s0, a = starts[e]
            s1 = jnp.minimum(next_ref[e * nt + j], s0 + SLOTS_PER_PASS)
            a128 = pl.multiple_of((a // 128) * 128, 128)
            seg = jnp.concatenate([ids[:, e * WIN:(e + 1) * WIN],
                                   jnp.zeros((1, FILL_SPAN - WIN), jnp.int32)], axis=1)
            seg = pltpu.roll(seg, a - a128, 1)
            cur = idx_ref[e:e + 1, pl.ds(a128, FILL_SPAN)]
            keep = jnp.logical_and(lane >= s0 - a128, lane < s1 - a128)
            idx_ref[e:e + 1, pl.ds(a128, FILL_SPAN)] = jnp.where(keep, seg, cur)
        return carry

    lax.fori_loop(0, npass_ref[j], one_pass, 0)


def _fill_call(lrank, base, cap):
    e, n = lrank.shape
    nt = n // RANK_TILE
    nxt = jnp.concatenate([base[:, 1:], jnp.full((e, 1), cap, jnp.int32)], axis=1)
    npass = jnp.maximum(1, (jnp.max(nxt - base, axis=0) + SLOTS_PER_PASS - 1) // SLOTS_PER_PASS).astype(jnp.int32)
    grid_spec = pltpu.PrefetchScalarGridSpec(
        num_scalar_prefetch=3, grid=(nt,),
        in_specs=[pl.BlockSpec((e, RANK_TILE), lambda j, *_: (0, j))],
        out_specs=pl.BlockSpec((e, cap + FILL_SPAN), lambda j, *_: (0, 0)))
    idx = pl.pallas_call(
        functools.partial(_fill_kernel, cap=cap), grid_spec=grid_spec,
        out_shape=jax.ShapeDtypeStruct((e, cap + FILL_SPAN), jnp.int32),
        compiler_params=_cparams("arbitrary"), name="slot_fill")(
            base.reshape(-1), nxt.reshape(-1), npass, lrank)
    return idx[:, :cap]


def _combine_kernel(base_ref, npass_ref, x_ref, lrank_ref, g_ref, b_ref, out_hbm, o_ref,
                    buf_ref, acc_ref, sem_ref, *, cap):
    j = pl.program_id(0)
    nt = pl.num_programs(0)

    def win_start(tile, e, p):
        r = base_ref[e * nt + tile] + p * SLOTS_PER_PASS
        return jnp.minimum((r // 16) * 16, cap - WIN)

    def window_copy(tile, e, p, slot):
        a = pl.multiple_of(win_start(tile, e, p), 16)
        return pltpu.make_async_copy(out_hbm.at[e, pl.ds(a, WIN), :],
                                     buf_ref.at[slot, pl.ds(e * WIN, WIN), :], sem_ref.at[slot])

    def start_all(tile, p, slot):
        for e in range(N_EXPERTS):
            window_copy(tile, e, p, slot).start()

    def wait_all(tile, p, slot):
        for e in range(N_EXPERTS):
            window_copy(tile, e, p, slot).wait()

    def add_pass(p, slot):
        row = lax.broadcasted_iota(jnp.int32, (WIN, RANK_TILE), 0)
        sel = []
        for e in range(N_EXPERTS):
            lr = lrank_ref[e:e + 1, :]
            off = base_ref[e * nt + j] - win_start(j, e, p)
            pos = jnp.where(jnp.logical_and(lr >= p * SLOTS_PER_PASS, lr < (p + 1) * SLOTS_PER_PASS),
                            lr + off, -1)
            sel.append(jnp.where(row == pos, 1.0, 0.0).astype(BF16))
        acc_ref[...] += _dot_tn(jnp.concatenate(sel, axis=0), buf_ref[slot])

    slot = j % 2

    @pl.when(j == 0)
    def _():
        start_all(0, 0, 0)

    @pl.when(j + 1 < nt)
    def _():
        start_all(j + 1, 0, 1 - slot)

    acc_ref[...] = jnp.zeros_like(acc_ref)
    wait_all(j, 0, slot)
    add_pass(0, slot)

    def extra(p, carry):
        start_all(j, p, slot)
        wait_all(j, p, slot)
        add_pass(p, slot)
        return carry

    lax.fori_loop(1, npass_ref[j], extra, 0)
    o_ref[...] = _layer_norm(ALPHA * x_ref[...] + acc_ref[...], g_ref[...], b_ref[...])


def _combine_call(x1, lrank, base, out, g, b):
    n, d = x1.shape
    e, cap, _ = out.shape
    nt = n // RANK_TILE
    assert cap >= WIN and cap % 16 == 0
    cnt = jnp.concatenate([base[:, 1:], jnp.full((e, 1), cap, jnp.int32)], axis=1) - base
    npass = jnp.maximum(1, (jnp.max(cnt, axis=0) + SLOTS_PER_PASS - 1) // SLOTS_PER_PASS).astype(jnp.int32)
    vec = pl.BlockSpec((1, d), lambda j, *_: (0, 0))
    grid_spec = pltpu.PrefetchScalarGridSpec(
        num_scalar_prefetch=2, grid=(nt,),
        in_specs=[pl.BlockSpec((RANK_TILE, d), lambda j, *_: (j, 0)),
                  pl.BlockSpec((e, RANK_TILE), lambda j, *_: (0, j)),
                  vec, vec, pl.BlockSpec(memory_space=pl.ANY)],
        out_specs=pl.BlockSpec((RANK_TILE, d), lambda j, *_: (j, 0)),
        scratch_shapes=[pltpu.VMEM((2, e * WIN, d), BF16), pltpu.VMEM((RANK_TILE, d), F32),
                        pltpu.SemaphoreType.DMA((2,))])
    return pl.pallas_call(
        functools.partial(_combine_kernel, cap=cap), grid_spec=grid_spec,
        out_shape=jax.ShapeDtypeStruct((n, d), F32),
        compiler_params=_cparams("arbitrary"), name="combine_ln")(
            base.reshape(-1), npass, x1, lrank, g.reshape(1, d), b.reshape(1, d), out)


def _hi_lo(w):
    hi = w.astype(BF16)
    lo = (w - hi.astype(F32)).astype(BF16)
    return jnp.concatenate([hi, lo], axis=-1)


def _lower_bounds(p):
    sm = jax.nn.softmax(p.astype(F32), axis=0)
    return jnp.cumsum(sm, axis=0) - sm[0:1]


def _trunk(x, prm):
    batch, seq, d = x.shape
    n = batch * seq
    rows = seq // GRID_W
    cap = EC_FACTOR * n // N_EXPERTS
    x = _ln_call(x.reshape(n, d), prm["ln_in_g"], prm["ln_in_b"])
    for l in range(DEPTH):
        qkv, gt, hf, hk = _inproj_call(x, prm["w_in"][l], prm["lb_fwd"][l], prm["lb_bwd"][l])
        na = _na_call(qkv, prm["na_bias"][l], batch, rows)
        o_f = _hgrn_call(hf, hk, batch, seq, reverse=False)
        hg = _hgrn_call(hf, hk, batch, seq, reverse=True, prev=o_f, norm_g=prm["hg_norm_g"][l])
        x1, x1b, aff = _mix_call(na, hg, gt, x, prm["w_branch_na"][l], prm["w_branch_hg"][l],
                                 prm["w_out"][l], prm["ln1_g"][l], prm["ln1_b"][l], prm["w_router2"][l])
        aff_t = aff.T
        tau, need = _tau_call(aff_t, cap)
        lrank, base = _rank_call(aff_t, tau, need)
        base = base[:, :, 0].T
        idx = _fill_call(lrank, base, cap)
        gate = jnp.take_along_axis(aff_t, idx, axis=1)
        xs = x1b[idx]
        out = _ffn_call(xs, prm["w_gate_e"], prm["w_up_e"], prm["w_down_e"], l, gate)
        x = _combine_call(x1, lrank, base, out, prm["ln2_g"][l], prm["ln2_b"][l])
    return x.reshape(batch, seq, d)


@jax.jit
def kernel(x_prompt, x_sample, ln_in_g, ln_in_b, w_in, na_rpb, hg_lb_fwd, hg_lb_bwd, hg_norm_g,
           w_branch_na, w_branch_hg, w_out, ln1_g, ln1_b, w_router, w_gate_e, w_up_e, w_down_e,
           ln2_g, ln2_b):
    prm = dict(
        ln_in_g=ln_in_g, ln_in_b=ln_in_b, w_in=w_in.astype(BF16), na_bias=_na_bias_tables(na_rpb),
        lb_fwd=_lower_bounds(hg_lb_fwd), lb_bwd=_lower_bounds(hg_lb_bwd), hg_norm_g=hg_norm_g,
        w_branch_na=w_branch_na.astype(BF16), w_branch_hg=w_branch_hg.astype(BF16),
        w_out=w_out.astype(BF16), ln1_g=ln1_g, ln1_b=ln1_b,
        w_router2=_hi_lo(w_router),
        w_gate_e=w_gate_e.astype(BF16), w_up_e=w_up_e.astype(BF16), w_down_e=w_down_e.astype(BF16),
        ln2_g=ln2_g, ln2_b=ln2_b)
    return _trunk(x_prompt, prm), _trunk(x_sample, prm)
```

```python
import functools

import jax
import jax.numpy as jnp
import numpy as np
from jax import lax
from jax.experimental import pallas as pl
from jax.experimental.pallas import tpu as pltpu

F32 = jnp.float32
BF16 = jnp.bfloat16

D_MODEL = 1024
DEPTH = 4
GRID_W = 64
NA_HEADS = 8
NA_HEAD_DIM = 64
NA_WIDTH = NA_HEADS * NA_HEAD_DIM
NA_KH = 8
NA_KW = 16
NA_ROWS = 8
NA_WIN = NA_KH + NA_ROWS - 1
HG_HEADS = 4
HG_HEAD_DIM = 128
HG_WIDTH = HG_HEADS * HG_HEAD_DIM
HG_CHUNK = 64
N_EXPERTS = 16
EC_FACTOR = 2
D_FF = 2 * D_MODEL
IN_WIDTH = 3 * NA_WIDTH + 5 * HG_WIDTH + 2 * D_MODEL
ALPHA = (2 * DEPTH) ** 0.25
LN_EPS = 1e-5
RMS_EPS = 1e-6
NEG_BIG = -1e30
F_MIN = 1e-30

GATE_WIDTH = 2 * D_MODEL
HF_WIDTH = 3 * HG_WIDTH
HK_WIDTH = 4 * HG_WIDTH
MIX_SPLIT = 2
INPROJ_CHUNK = 256
HG_LEVELS = (32, 16, 8, 4, 2, 1)
HG_GROUP = 8
HG_TRI = 4
VMEM_LIMIT = 48 * 1024 * 1024
RANK_TILE = 256
RANK_STEP = 4
SLOTS_PER_PASS = 48
WIN = SLOTS_PER_PASS + 16
TAU_CHUNK = 2048
FILL_SPAN = 256


def _cparams(*sem):
    return pltpu.CompilerParams(dimension_semantics=sem, vmem_limit_bytes=VMEM_LIMIT)


def _sigmoid(x):
    return 0.5 * jnp.tanh(0.5 * x) + 0.5


def _layer_norm(x, g, b):
    mu = jnp.mean(x, axis=-1, keepdims=True)
    xc = x - mu
    var = jnp.mean(xc * xc, axis=-1, keepdims=True)
    return xc * lax.rsqrt(var + LN_EPS) * g + b


def _dot(a, b):
    return jnp.dot(a, b, preferred_element_type=F32)


def _dot_nt(a, b):
    return lax.dot_general(a, b, (((1,), (1,)), ((), ())), preferred_element_type=F32)


def _dot_tn(a, b):
    return lax.dot_general(a, b, (((0,), (0,)), ((), ())), preferred_element_type=F32)


def _inproj_kernel(*refs, norm):
    if norm:
        x_ref, w_ref, lbf_ref, lbb_ref, g_ref, b_ref, qkv_ref, gt_ref, hf_ref, hk_ref, xn_ref = refs
        x = _layer_norm(x_ref[...], g_ref[...], b_ref[...])
        xn_ref[...] = x
    else:
        x_ref, w_ref, lbf_ref, lbb_ref, qkv_ref, gt_ref, hf_ref, hk_ref = refs
        x = x_ref[...]
    xb = x.astype(BF16)
    cw = INPROJ_CHUNK
    for c in range(IN_WIDTH // cw):
        acc = _dot(xb, w_ref[:, c * cw:(c + 1) * cw])
        o = c * cw
        if o < 3 * NA_WIDTH:
            if o < NA_WIDTH:
                acc = acc * (NA_HEAD_DIM ** -0.5)
            qkv_ref[:, o:o + cw] = acc.astype(BF16)
        elif o < 3 * NA_WIDTH + 5 * HG_WIDTH:
            o -= 3 * NA_WIDTH
            s, o = o // HG_WIDTH, o % HG_WIDTH
            if s == 0:
                hf_ref[:, o:o + cw] = acc * _sigmoid(acc)
            elif s in (1, 2):
                lb = (lbf_ref if s == 1 else lbb_ref)[:, o:o + cw]
                sg = _sigmoid(acc)
                f = lb + (1.0 - lb) * sg
                hf_ref[:, s * HG_WIDTH + o:s * HG_WIDTH + o + cw] = jnp.log(jnp.maximum(f, F_MIN))
                hk_ref[:, (s - 1) * HG_WIDTH + o:(s - 1) * HG_WIDTH + o + cw] = (
                    (1.0 - lb) * (1.0 - sg)).astype(BF16)
            elif s == 3:
                hk_ref[:, 2 * HG_WIDTH + o:2 * HG_WIDTH + o + cw] = acc.astype(BF16)
            else:
                hk_ref[:, 3 * HG_WIDTH + o:3 * HG_WIDTH + o + cw] = (acc * _sigmoid(acc)).astype(BF16)
        else:
            o -= 3 * NA_WIDTH + 5 * HG_WIDTH
            gt_ref[:, o:o + cw] = _sigmoid(acc).astype(BF16)


def _inproj_call(x, w, lb_fwd, lb_bwd, ln=None, tm=256):
    n, d = x.shape
    vec = pl.BlockSpec((1, HG_WIDTH), lambda i: (0, 0))
    row = pl.BlockSpec((tm, d), lambda i: (i, 0))
    in_specs = [row, pl.BlockSpec((d, IN_WIDTH), lambda i: (0, 0), pipeline_mode=pl.Buffered(1)), vec, vec]
    args = [x, w, lb_fwd.reshape(1, HG_WIDTH), lb_bwd.reshape(1, HG_WIDTH)]
    out_specs = [pl.BlockSpec((tm, 3 * NA_WIDTH), lambda i: (i, 0)),
                 pl.BlockSpec((tm, GATE_WIDTH), lambda i: (i, 0)),
                 pl.BlockSpec((tm, HF_WIDTH), lambda i: (i, 0)),
                 pl.BlockSpec((tm, HK_WIDTH), lambda i: (i, 0))]
    out_shape = [jax.ShapeDtypeStruct((n, 3 * NA_WIDTH), BF16),
                 jax.ShapeDtypeStruct((n, GATE_WIDTH), BF16),
                 jax.ShapeDtypeStruct((n, HF_WIDTH), F32),
                 jax.ShapeDtypeStruct((n, HK_WIDTH), BF16)]
    if ln is not None:
        in_specs += [pl.BlockSpec((1, d), lambda i: (0, 0))] * 2
        args += [ln[0].reshape(1, d), ln[1].reshape(1, d)]
        out_specs.append(row)
        out_shape.append(jax.ShapeDtypeStruct((n, d), F32))
    return pl.pallas_call(
        functools.partial(_inproj_kernel, norm=ln is not None), grid=(n // tm,),
        in_specs=in_specs, out_specs=out_specs, out_shape=out_shape,
        compiler_params=_cparams("parallel"), name="in_proj")(*args)


def _na_kernel(*refs, rows):
    q_ref, k_ref, v_ref = refs[:3]
    bias_refs = refs[3:3 + NA_ROWS]
    o_ref, s_ref = refs[3 + NA_ROWS:]
    r0 = pl.program_id(1) * NA_ROWS
    w0 = jnp.minimum(_na_start(r0, rows), rows - NA_WIN)
    lane = lax.broadcasted_iota(jnp.int32, (1, 128), 1)
    lo = lane < NA_HEAD_DIM
    hi = jnp.logical_not(lo)
    offs = [pl.multiple_of((_na_start(r0 + i, rows) - w0) * GRID_W, GRID_W) for i in range(NA_ROWS)]
    for i in range(NA_ROWS):
        qr = slice(i * GRID_W, (i + 1) * GRID_W)
        kr = pl.ds(offs[i], NA_KH * GRID_W)
        for p in range(NA_HEADS // 2):
            cs = slice(p * 128, (p + 1) * 128)
            q2 = q_ref[qr, cs]
            k2 = k_ref[kr, cs]
            for hh in range(2):
                qm = jnp.where(hi if hh else lo, q2, jnp.zeros_like(q2))
                s = _dot_nt(qm, k2) + bias_refs[i][2 * p + hh]
                s_ref[i * NA_HEADS + 2 * p + hh] = s - jnp.max(s, axis=-1, keepdims=True)
    for i in range(NA_ROWS):
        qr = slice(i * GRID_W, (i + 1) * GRID_W)
        kr = pl.ds(offs[i], NA_KH * GRID_W)
        for p in range(NA_HEADS // 2):
            cs = slice(p * 128, (p + 1) * 128)
            v2 = v_ref[kr, cs]
            res = []
            for hh in range(2):
                e = jnp.exp(s_ref[i * NA_HEADS + 2 * p + hh]).astype(BF16)
                va = jnp.where(hi if hh else lo, v2, jnp.ones_like(v2))
                res.append(_dot(e, va))
            o = jnp.where(lo, res[0], res[1])
            l = pltpu.roll(jnp.where(lo, res[1], res[0]), NA_HEAD_DIM, 1)
            o_ref[qr, cs] = (o / l).astype(BF16)


def _na_start(r, rows):
    return jnp.clip(r - NA_KH // 2, 0, rows - NA_KH)


def _na_call(qkv, bias, batch, rows):
    n = qkv.shape[0]
    assert rows >= NA_WIN and rows % NA_ROWS == 0
    steps = rows // NA_ROWS

    def win_tok(b, j):
        w0 = jnp.minimum(_na_start(j * NA_ROWS, rows), rows - NA_WIN)
        return (b * rows + w0) * GRID_W

    def bias_spec(i):
        return pl.BlockSpec((None, NA_HEADS, GRID_W, NA_KH * GRID_W),
                            lambda b, j: (j * NA_ROWS + i - _na_start(j * NA_ROWS + i, rows), 0, 0, 0))

    win = (pl.Element(NA_WIN * GRID_W), pl.Element(NA_WIDTH))
    blk = (NA_ROWS * GRID_W, NA_WIDTH)
    return pl.pallas_call(
        functools.partial(_na_kernel, rows=rows), grid=(batch, steps),
        in_specs=[pl.BlockSpec(blk, lambda b, j: (b * steps + j, 0)),
                  pl.BlockSpec(win, lambda b, j: (win_tok(b, j), NA_WIDTH)),
                  pl.BlockSpec(win, lambda b, j: (win_tok(b, j), 2 * NA_WIDTH))]
                 + [bias_spec(i) for i in range(NA_ROWS)],
        out_specs=pl.BlockSpec(blk, lambda b, j: (b * steps + j, 0)),
        out_shape=jax.ShapeDtypeStruct((n, NA_WIDTH), BF16),
        scratch_shapes=[pltpu.VMEM((NA_ROWS * NA_HEADS, GRID_W, NA_KH * GRID_W), F32)],
        compiler_params=_cparams("parallel", "arbitrary"), name="na_attn")(
            qkv, qkv, qkv, *([bias] * NA_ROWS))


def _na_bias_tables(rpb):
    c = np.arange(GRID_W)
    c_start = np.clip(c - NA_KW // 2, 0, GRID_W - NA_KW)
    col_mask = (c[None, :] >= c_start[:, None]) & (c[None, :] < c_start[:, None] + NA_KW)
    dc = np.clip(c[None, :] - c[:, None], -(NA_KW - 1), NA_KW - 1) + (NA_KW - 1)
    onehot = jnp.asarray(dc[:, :, None] == np.arange(2 * NA_KW - 1), F32)
    t = jnp.einsum("lhab,qkb->lhqak", rpb.astype(F32), onehot, precision=lax.Precision.HIGHEST)
    t = jnp.where(col_mask[None, None, :, None, :], t, NEG_BIG)
    per_cls = [t[:, :, :, NA_KH - 1 - cls:2 * NA_KH - 1 - cls] for cls in range(NA_KH)]
    t = jnp.stack(per_cls, axis=1)
    return t.reshape(rpb.shape[0], NA_KH, NA_HEADS, GRID_W, NA_KH * GRID_W)


def _hg_constants(reverse):
    n = HG_CHUNK
    t = np.arange(n)
    if reverse:
        tri = (t[None, :] >= t[:, None])
    else:
        tri = (t[None, :] <= t[:, None])
    masks = []
    for c in HG_LEVELS:
        blk = t // (2 * c)
        upper = (t % (2 * c)) >= c
        same = blk[:, None] == blk[None, :]
        if reverse:
            m = same & (~upper)[:, None] & upper[None, :]
        else:
            m = same & upper[:, None] & (~upper)[None, :]
        masks.append(m)
    masks.append(np.eye(n, dtype=bool))
    tri = np.kron(np.eye(HG_TRI, dtype=bool), tri)
    return jnp.asarray(tri, BF16), jnp.asarray(np.stack(masks), F32)


def _hgrn_kernel(*refs, reverse, final):
    if final:
        q_ref, g_ref, k_ref, v_ref, tri_ref, lm_ref, gate_ref, prev_ref, ng_ref, o_ref, st_ref = refs
    else:
        q_ref, g_ref, k_ref, v_ref, tri_ref, lm_ref, o_ref, st_ref = refs
    n = HG_CHUNK
    grp = HG_GROUP
    w = HG_WIDTH
    m = grp * n

    @pl.when(pl.program_id(1) == 0)
    def _():
        st_ref[...] = jnp.zeros_like(st_ref)

    g = g_ref[...]
    kk = k_ref[...].astype(F32)
    qq = q_ref[...]
    v_b = v_ref[...]

    g1 = g.astype(BF16)
    r1 = g - g1.astype(F32)
    g2 = r1.astype(BF16)
    g3 = (r1 - g2.astype(F32)).astype(BF16)
    tri = tri_ref[...]
    tr = tri.shape[0]
    b = jnp.concatenate([_dot(tri, g1[r:r + tr]) + _dot(tri, g2[r:r + tr]) + _dot(tri, g3[r:r + tr])
                         for r in range(0, m, tr)], axis=0)
    last = 0 if reverse else n - 1
    bc = b.reshape(grp, n, w)
    b_last = bc[:, last:last + 1, :]
    e_last = jnp.exp(b_last)

    row = lax.broadcasted_iota(jnp.int32, (m, 1), 0)
    qs, ks = [], []
    for c in HG_LEVELS:
        ref_row = c if reverse else c - 1
        if c == 1:
            d = jnp.where((row % 2) == (0 if reverse else 1), g, 0.0)
        elif c == 2:
            b3 = b.reshape(m // 8, 8, w)
            sub = lax.broadcasted_iota(jnp.int32, (1, 8, 1), 1)
            r_lo = jnp.broadcast_to(b3[:, ref_row:ref_row + 1, :], b3.shape)
            r_hi = jnp.broadcast_to(b3[:, ref_row + 4:ref_row + 5, :], b3.shape)
            d = -jnp.abs(b3 - jnp.where(sub < 4, r_lo, r_hi)).reshape(m, w)
        else:
            b3 = b.reshape(m // (2 * c), 2 * c, w)
            d = -jnp.abs(b3 - jnp.broadcast_to(b3[:, ref_row:ref_row + 1, :], b3.shape)).reshape(m, w)
        e = jnp.exp(d)
        qs.append((qq * e).astype(BF16))
        ks.append((kk * e).astype(BF16))
    qs.append(qq.astype(BF16))
    ks.append(kk.astype(BF16))
    q_in = (qq * jnp.exp(b)).astype(BF16)
    k_out = (kk * jnp.exp(b_last - bc).reshape(m, w)).astype(BF16)

    intra = {}
    for ci in range(grp):
        rs = slice(ci * n, (ci + 1) * n)
        for h in range(HG_HEADS):
            hs = slice(h * HG_HEAD_DIM, (h + 1) * HG_HEAD_DIM)
            a = None
            for li in range(len(qs)):
                term = lm_ref[li] * _dot_nt(qs[li][rs, hs], ks[li][rs, hs])
                a = term if a is None else a + term
            intra[ci, h] = _dot(a.astype(BF16), v_b[rs, hs])

    if final:
        gs = gate_ref[...].astype(F32)
    for ci in (range(grp - 1, -1, -1) if reverse else range(grp)):
        rs = slice(ci * n, (ci + 1) * n)
        for h in range(HG_HEADS):
            hs = slice(h * HG_HEAD_DIM, (h + 1) * HG_HEAD_DIM)
            st = st_ref[h]
            o = intra[ci, h] + _dot_nt(q_in[rs, hs], st.astype(BF16))
            st_ref[h] = st * e_last[ci][:, hs] + _dot_tn(v_b[rs, hs], k_out[rs, hs])
            if final:
                o = o + prev_ref[rs, hs]
                o = o * lax.rsqrt(jnp.mean(o * o, axis=-1, keepdims=True) + RMS_EPS) * ng_ref[:, hs]
                o_ref[rs, hs] = (o * gs[rs, hs]).astype(BF16)
            else:
                o_ref[rs, hs] = o


def _hgrn_call(hf, hk, batch, seq, reverse, prev=None, norm_g=None):
    n = hf.shape[0]
    rows = HG_GROUP * HG_CHUNK
    assert seq % rows == 0
    nc = seq // rows
    final = prev is not None
    tri, masks = _hg_constants(reverse)

    def tok(b, c):
        return b * nc + ((nc - 1 - c) if reverse else c)

    def col(j):
        return pl.BlockSpec((rows, HG_WIDTH), lambda b, c: (tok(b, c), j))

    vec = pl.BlockSpec((1, HG_WIDTH), lambda b, c: (0, 0))
    d = 1 if reverse else 0
    in_specs = [col(0), col(1 + d), col(d), col(2),
                pl.BlockSpec(tri.shape, lambda b, c: (0, 0)),
                pl.BlockSpec(masks.shape, lambda b, c: (0, 0, 0))]
    args = [hf, hf, hk, hk, tri, masks]
    if final:
        in_specs += [col(3), col(0), vec]
        args += [hk, prev, norm_g.reshape(1, HG_WIDTH)]
    return pl.pallas_call(
        functools.partial(_hgrn_kernel, reverse=reverse, final=final),
        grid=(batch, nc), in_specs=in_specs,
        out_specs=pl.BlockSpec((rows, HG_WIDTH), lambda b, c: (tok(b, c), 0)),
        out_shape=jax.ShapeDtypeStruct((n, HG_WIDTH), BF16 if final else F32),
        scratch_shapes=[pltpu.VMEM((HG_HEADS, HG_HEAD_DIM, HG_HEAD_DIM), F32)],
        compiler_params=_cparams("arbitrary", "arbitrary"),
        name="hgrn_bwd" if reverse else "hgrn_fwd")(*args)


def _mix_kernel(na_ref, hg_ref, gt_ref, x_ref, wna_ref, whg_ref, wout_ref, g_ref, b_ref, wr_ref,
                x1_ref, x1b_ref, aff_ref):
    hm = x_ref.shape[0] // MIX_SPLIT
    for s in range(MIX_SPLIT):
        rs = slice(s * hm, (s + 1) * hm)
        a = _dot(na_ref[rs, :], wna_ref[...])
        r = _dot(hg_ref[rs, :], whg_ref[...])
        mix = gt_ref[rs, :D_MODEL].astype(F32) * a + gt_ref[rs, D_MODEL:].astype(F32) * r
        y = _dot(mix.astype(BF16), wout_ref[...])
        x1 = _layer_norm(ALPHA * x_ref[rs, :] + y, g_ref[...], b_ref[...])
        x1_ref[rs, :] = x1
        x_hi = x1.astype(BF16)
        x1b_ref[rs, :] = x_hi
        x_lo = (x1 - x_hi.astype(F32)).astype(BF16)
        p = _dot(x_hi, wr_ref[...]) + _dot(x_lo, wr_ref[...])
        logits = p[:, :N_EXPERTS] + p[:, N_EXPERTS:]
        e = jnp.exp(logits - jnp.max(logits, axis=-1, keepdims=True))
        aff_ref[rs, :] = e / jnp.sum(e, axis=-1, keepdims=True)


def _mix_call(na, hg, gt, x, wna, whg, wout, g, b, wr2, tm=512):
    n, d = x.shape
    const = lambda shape: pl.BlockSpec(shape, lambda i: (0,) * len(shape))
    return pl.pallas_call(
        _mix_kernel, grid=(n // tm,),
        in_specs=[pl.BlockSpec((tm, NA_WIDTH), lambda i: (i, 0)),
                  pl.BlockSpec((tm, HG_WIDTH), lambda i: (i, 0)),
                  pl.BlockSpec((tm, 2 * D_MODEL), lambda i: (i, 0)),
                  pl.BlockSpec((tm, d), lambda i: (i, 0)),
                  const((NA_WIDTH, d)), const((HG_WIDTH, d)), const((d, d)),
                  const((1, d)), const((1, d)), const((d, 2 * N_EXPERTS))],
        out_specs=[pl.BlockSpec((tm, d), lambda i: (i, 0)),
                   pl.BlockSpec((tm, d), lambda i: (i, 0)),
                   pl.BlockSpec((tm, N_EXPERTS), lambda i: (i, 0))],
        out_shape=[jax.ShapeDtypeStruct((n, d), F32), jax.ShapeDtypeStruct((n, d), BF16),
                   jax.ShapeDtypeStruct((n, N_EXPERTS), F32)],
        compiler_params=_cparams("parallel"), name="mix_router")(
            na, hg, gt, x, wna, whg, wout, g.reshape(1, d), b.reshape(1, d), wr2)


def _ffn_kernel(xs_ref, wg_ref, wu_ref, wd_ref, gate_ref, o_ref, acc_ref):
    k = pl.program_id(2)
    xs = xs_ref[...]
    hg = _dot(xs, wg_ref[...])
    hu = _dot(xs, wu_ref[...])
    part = _dot((hg * _sigmoid(hg) * hu).astype(BF16), wd_ref[...])

    @pl.when(k == 0)
    def _():
        acc_ref[...] = part

    @pl.when(k > 0)
    def _():
        acc_ref[...] += part

    @pl.when(k == pl.num_programs(2) - 1)
    def _():
        o_ref[...] = (acc_ref[...] * gate_ref[...]).astype(BF16)


def _ffn_call(xs, wg, wu, wd, layer, gate, tc=1024, tf=1024):
    e, c, d = xs.shape
    f = wg.shape[-1]
    tc = min(tc, c)
    return pl.pallas_call(
        _ffn_kernel, grid=(e, c // tc, f // tf),
        in_specs=[pl.BlockSpec((None, tc, d), lambda i, j, k: (i, j, 0)),
                  pl.BlockSpec((None, None, d, tf), lambda i, j, k: (layer, i, 0, k)),
                  pl.BlockSpec((None, None, d, tf), lambda i, j, k: (layer, i, 0, k)),
                  pl.BlockSpec((None, None, tf, d), lambda i, j, k: (layer, i, k, 0)),
                  pl.BlockSpec((None, tc, 1), lambda i, j, k: (i, j, 0))],
        out_specs=pl.BlockSpec((None, tc, d), lambda i, j, k: (i, j, 0)),
        out_shape=jax.ShapeDtypeStruct((e, c, d), BF16),
        scratch_shapes=[pltpu.VMEM((tc, d), F32)],
        compiler_params=_cparams("parallel", "parallel", "arbitrary"), name="expert_ffn")(
            xs, wg, wu, wd, gate.reshape(e, c, 1))


def _rank_kernel(aff_ref, tau_ref, need_ref, lrank_ref, base_ref, ceq_ref, csel_ref):
    @pl.when(pl.program_id(0) == 0)
    def _():
        ceq_ref[...] = jnp.zeros_like(ceq_ref)
        csel_ref[...] = jnp.zeros_like(csel_ref)

    tau = tau_ref[:, 0:1]
    u = lax.broadcasted_iota(jnp.int32, (RANK_TILE, RANK_TILE), 0)
    t = lax.broadcasted_iota(jnp.int32, (RANK_TILE, RANK_TILE), 1)
    before = jnp.where(u < t, 1.0, 0.0).astype(BF16)
    ones = jnp.ones((RANK_TILE, 128), BF16)
    for i in range(RANK_STEP):
        ts = slice(i * RANK_TILE, (i + 1) * RANK_TILE)
        a = aff_ref[:, ts]
        eq = a == tau
        eq_f = jnp.where(eq, 1.0, 0.0).astype(BF16)
        eq_before = ceq_ref[:, 0:1] + _dot(eq_f, before)
        sel = jnp.logical_or(a > tau, jnp.logical_and(eq, eq_before < need_ref[:, 0:1]))
        sel_f = jnp.where(sel, 1.0, 0.0).astype(BF16)
        lrank_ref[:, ts] = jnp.where(sel, _dot(sel_f, before), -1.0).astype(jnp.int32)
        base_ref[i] = csel_ref[...].astype(jnp.int32)
        ceq_ref[...] += _dot(eq_f, ones)
        csel_ref[...] += _dot(sel_f, ones)


def _rank_call(aff_t, tau, need):
    e, n = aff_t.shape
    nt = n // RANK_TILE
    step = RANK_STEP * RANK_TILE
    assert n % step == 0
    vec = pl.BlockSpec((e, 128), lambda j: (0, 0))
    return pl.pallas_call(
        _rank_kernel, grid=(n // step,),
        in_specs=[pl.BlockSpec((e, step), lambda j: (0, j)), vec, vec],
        out_specs=[pl.BlockSpec((e, step), lambda j: (0, j)),
                   pl.BlockSpec((RANK_STEP, e, 128), lambda j: (j, 0, 0))],
        out_shape=[jax.ShapeDtypeStruct((e, n), jnp.int32),
                   jax.ShapeDtypeStruct((nt, e, 128), jnp.int32)],
        scratch_shapes=[pltpu.VMEM((e, 128), F32), pltpu.VMEM((e, 128), F32)],
        compiler_params=_cparams("arbitrary"), name="slot_rank")(aff_t, tau, need)


def _tau_kernel(aff_ref, tau_ref, need_ref, *, cap):
    e, n = aff_ref.shape

    def count_ge(cand):
        def body(i, acc):
            c0 = pl.multiple_of(i * TAU_CHUNK, TAU_CHUNK)
            bits = pltpu.bitcast(aff_ref[:, pl.ds(c0, TAU_CHUNK)], jnp.int32)
            hit = jnp.where(bits >= cand, 1.0, 0.0)
            for k in range(TAU_CHUNK // 128):
                acc = acc + hit[:, k * 128:(k + 1) * 128]
            return acc
        acc = lax.fori_loop(0, n // TAU_CHUNK, body, jnp.zeros((e, 128), F32))
        return jnp.sum(acc, axis=1, keepdims=True)

    lo = jnp.zeros((e, 1), jnp.int32)
    for bit in range(30, -1, -1):
        cand = lo | (1 << bit)
        lo = jnp.where(count_ge(cand) >= cap, cand, lo)
    tau_ref[...] = jnp.broadcast_to(pltpu.bitcast(lo, F32), (e, 128))
    need_ref[...] = jnp.broadcast_to(cap - count_ge(lo + 1), (e, 128))


def _tau_call(aff_t, cap):
    e, n = aff_t.shape
    assert n % TAU_CHUNK == 0
    full = pl.BlockSpec((e, n), lambda i: (0, 0))
    vec = pl.BlockSpec((e, 128), lambda i: (0, 0))
    return pl.pallas_call(
        functools.partial(_tau_kernel, cap=cap), grid=(1,), in_specs=[full], out_specs=[vec, vec],
        out_shape=[jax.ShapeDtypeStruct((e, 128), F32)] * 2,
        compiler_params=_cparams("arbitrary"), name="route_tau")(aff_t)


def _fill_kernel(base_ref, next_ref, npass_ref, lrank_ref, idx_ref, *, cap):
    j = pl.program_id(0)
    nt = pl.num_programs(0)

    @pl.when(j == 0)
    def _():
        idx_ref[...] = jnp.zeros_like(idx_ref)

    tok = j * RANK_TILE + lax.broadcasted_iota(jnp.int32, (8, RANK_TILE), 1)
    sub = lax.broadcasted_iota(jnp.int32, (8, RANK_TILE), 0)
    vals = jnp.where(sub == 0, tok // 256, jnp.where(sub == 1, tok % 256, 0)).astype(F32).astype(BF16)
    row = lax.broadcasted_iota(jnp.int32, (WIN, RANK_TILE), 0)
    lane = lax.broadcasted_iota(jnp.int32, (1, FILL_SPAN), 1)

    def one_pass(p, carry):
        starts, sel = [], []
        for e in range(N_EXPERTS):
            s0 = base_ref[e * nt + j] + p * SLOTS_PER_PASS
            a = jnp.minimum((s0 // 16) * 16, cap - WIN)
            starts.append((s0, a))
            lr = lrank_ref[e:e + 1, :]
            pos = jnp.where(jnp.logical_and(lr >= p * SLOTS_PER_PASS, lr < (p + 1) * SLOTS_PER_PASS),
                            lr - p * SLOTS_PER_PASS + (s0 - a), -1)
            sel.append(jnp.where(row == pos, 1.0, 0.0).astype(BF16))
        got = _dot_nt(vals, jnp.concatenate(sel, axis=0))
        ids = (got[0:1, :] * 256.0 + got[1:2, :]).astype(jnp.int32)
        for e in range(N_EXPERTS):
            s0, a = starts[e]
            s1 = jnp.minimum(next_ref[e * nt + j], s0 + SLOTS_PER_PASS)
            a128 = pl.multiple_of((a // 128) * 128, 128)
            seg = jnp.concatenate([ids[:, e * WIN:(e + 1) * WIN],
                                   jnp.zeros((1, FILL_SPAN - WIN), jnp.int32)], axis=1)
            seg = pltpu.roll(seg, a - a128, 1)
            cur = idx_ref[e:e + 1, pl.ds(a128, FILL_SPAN)]
            keep = jnp.logical_and(lane >= s0 - a128, lane < s1 - a128)
            idx_ref[e:e + 1, pl.ds(a128, FILL_SPAN)] = jnp.where(keep, seg, cur)
        return carry

    lax.fori_loop(0, npass_ref[j], one_pass, 0)


def _fill_call(lrank, base, cap):
    e, n = lrank.shape
    nt = n // RANK_TILE
    nxt = jnp.concatenate([base[:, 1:], jnp.full((e, 1), cap, jnp.int32)], axis=1)
    npass = jnp.maximum(1, (jnp.max(nxt - base, axis=0) + SLOTS_PER_PASS - 1) // SLOTS_PER_PASS).astype(jnp.int32)
    grid_spec = pltpu.PrefetchScalarGridSpec(
        num_scalar_prefetch=3, grid=(nt,),
        in_specs=[pl.BlockSpec((e, RANK_TILE), lambda j, *_: (0, j))],
        out_specs=pl.BlockSpec((e, cap + FILL_SPAN), lambda j, *_: (0, 0)))
    idx = pl.pallas_call(
        functools.partial(_fill_kernel, cap=cap), grid_spec=grid_spec,
        out_shape=jax.ShapeDtypeStruct((e, cap + FILL_SPAN), jnp.int32),
        compiler_params=_cparams("arbitrary"), name="slot_fill")(
            base.reshape(-1), nxt.reshape(-1), npass, lrank)
    return idx[:, :cap]


def _combine_kernel(base_ref, npass_ref, x_ref, lrank_ref, g_ref, b_ref, out_hbm, o_ref,
                    buf_ref, acc_ref, sem_ref, *, cap):
    j = pl.program_id(0)
    nt = pl.num_programs(0)

    def win_start(tile, e, p):
        r = base_ref[e * nt + tile] + p * SLOTS_PER_PASS
        return jnp.minimum((r // 16) * 16, cap - WIN)

    def window_copy(tile, e, p, slot):
        a = pl.multiple_of(win_start(tile, e, p), 16)
        return pltpu.make_async_copy(out_hbm.at[e, pl.ds(a, WIN), :],
                                     buf_ref.at[slot, pl.ds(e * WIN, WIN), :], sem_ref.at[slot])

    def start_all(tile, p, slot):
        for e in range(N_EXPERTS):
            window_copy(tile, e, p, slot).start()

    def wait_all(tile, p, slot):
        for e in range(N_EXPERTS):
            window_copy(tile, e, p, slot).wait()

    def add_pass(p, slot):
        row = lax.broadcasted_iota(jnp.int32, (WIN, RANK_TILE), 0)
        sel = []
        for e in range(N_EXPERTS):
            lr = lrank_ref[e:e + 1, :]
            off = base_ref[e * nt + j] - win_start(j, e, p)
            pos = jnp.where(jnp.logical_and(lr >= p * SLOTS_PER_PASS, lr < (p + 1) * SLOTS_PER_PASS),
                            lr + off, -1)
            sel.append(jnp.where(row == pos, 1.0, 0.0).astype(BF16))
        acc_ref[...] += _dot_tn(jnp.concatenate(sel, axis=0), buf_ref[slot])

    slot = j % 2

    @pl.when(j == 0)
    def _():
        start_all(0, 0, 0)

    @pl.when(j + 1 < nt)
    def _():
        start_all(j + 1, 0, 1 - slot)

    acc_ref[...] = jnp.zeros_like(acc_ref)
    wait_all(j, 0, slot)
    add_pass(0, slot)

    def extra(p, carry):
        start_all(j, p, slot)
        wait_all(j, p, slot)
        add_pass(p, slot)
        return carry

    lax.fori_loop(1, npass_ref[j], extra, 0)
    o_ref[...] = _layer_norm(ALPHA * x_ref[...] + acc_ref[...], g_ref[...], b_ref[...])


def _combine_call(x1, lrank, base, out, g, b):
    n, d = x1.shape
    e, cap, _ = out.shape
    nt = n // RANK_TILE
    assert cap >= WIN and cap % 16 == 0
    cnt = jnp.concatenate([base[:, 1:], jnp.full((e, 1), cap, jnp.int32)], axis=1) - base
    npass = jnp.maximum(1, (jnp.max(cnt, axis=0) + SLOTS_PER_PASS - 1) // SLOTS_PER_PASS).astype(jnp.int32)
    vec = pl.BlockSpec((1, d), lambda j, *_: (0, 0))
    grid_spec = pltpu.PrefetchScalarGridSpec(
        num_scalar_prefetch=2, grid=(nt,),
        in_specs=[pl.BlockSpec((RANK_TILE, d), lambda j, *_: (j, 0)),
                  pl.BlockSpec((e, RANK_TILE), lambda j, *_: (0, j)),
                  vec, vec, pl.BlockSpec(memory_space=pl.ANY)],
        out_specs=pl.BlockSpec((RANK_TILE, d), lambda j, *_: (j, 0)),
        scratch_shapes=[pltpu.VMEM((2, e * WIN, d), BF16), pltpu.VMEM((RANK_TILE, d), F32),
                        pltpu.SemaphoreType.DMA((2,))])
    return pl.pallas_call(
        functools.partial(_combine_kernel, cap=cap), grid_spec=grid_spec,
        out_shape=jax.ShapeDtypeStruct((n, d), F32),
        compiler_params=_cparams("arbitrary"), name="combine_ln")(
            base.reshape(-1), npass, x1, lrank, g.reshape(1, d), b.reshape(1, d), out)


def _hi_lo(w):
    hi = w.astype(BF16)
    lo = (w - hi.astype(F32)).astype(BF16)
    return jnp.concatenate([hi, lo], axis=-1)


def _lower_bounds(p):
    sm = jax.nn.softmax(p.astype(F32), axis=0)
    return jnp.cumsum(sm, axis=0) - sm[0:1]


def _trunk(x, prm):
    batch, seq, d = x.shape
    n = batch * seq
    rows = seq // GRID_W
    cap = EC_FACTOR * n // N_EXPERTS
    x = x.reshape(n, d)
    for l in range(DEPTH):
        if l == 0:
            qkv, gt, hf, hk, x = _inproj_call(x, prm["w_in"][l], prm["lb_fwd"][l], prm["lb_bwd"][l],
                                              ln=(prm["ln_in_g"], prm["ln_in_b"]))
        else:
            qkv, gt, hf, hk = _inproj_call(x, prm["w_in"][l], prm["lb_fwd"][l], prm["lb_bwd"][l])
        na = _na_call(qkv, prm["na_bias"][l], batch, rows)
        o_f = _hgrn_call(hf, hk, batch, seq, reverse=False)
        hg = _hgrn_call(hf, hk, batch, seq, reverse=True, prev=o_f, norm_g=prm["hg_norm_g"][l])
        x1, x1b, aff = _mix_call(na, hg, gt, x, prm["w_branch_na"][l], prm["w_branch_hg"][l],
                                 prm["w_out"][l], prm["ln1_g"][l], prm["ln1_b"][l], prm["w_router2"][l])
        aff_t = aff.T
        tau, need = _tau_call(aff_t, cap)
        lrank, base = _rank_call(aff_t, tau, need)
        base = base[:, :, 0].T
        idx = _fill_call(lrank, base, cap)
        gate = jnp.take_along_axis(aff_t, idx, axis=1)
        xs = x1b[idx]
        out = _ffn_call(xs, prm["w_gate_e"], prm["w_up_e"], prm["w_down_e"], l, gate)
        x = _combine_call(x1, lrank, base, out, prm["ln2_g"][l], prm["ln2_b"][l])
    return x.reshape(batch, seq, d)


@jax.jit
def kernel(x_prompt, x_sample, ln_in_g, ln_in_b, w_in, na_rpb, hg_lb_fwd, hg_lb_bwd, hg_norm_g,
           w_branch_na, w_branch_hg, w_out, ln1_g, ln1_b, w_router, w_gate_e, w_up_e, w_down_e,
           ln2_g, ln2_b):
    prm = dict(
        ln_in_g=ln_in_g, ln_in_b=ln_in_b, w_in=w_in.astype(BF16), na_bias=_na_bias_tables(na_rpb),
        lb_fwd=_lower_bounds(hg_lb_fwd), lb_bwd=_lower_bounds(hg_lb_bwd), hg_norm_g=hg_norm_g,
        w_branch_na=w_branch_na.astype(BF16), w_branch_hg=w_branch_hg.astype(BF16),
        w_out=w_out.astype(BF16), ln1_g=ln1_g, ln1_b=ln1_b,
        w_router2=_hi_lo(w_router),
        w_gate_e=w_gate_e.astype(BF16), w_up_e=w_up_e.astype(BF16), w_down_e=w_down_e.astype(BF16),
        ln2_g=ln2_g, ln2_b=ln2_b)
    return _trunk(x_prompt, prm), _trunk(x_sample, prm)
```

```python
import functools

import jax
import jax.numpy as jnp
import numpy as np
from jax import lax
from jax.experimental import pallas as pl
from jax.experimental.pallas import tpu as pltpu

F32 = jnp.float32
BF16 = jnp.bfloat16

D_MODEL = 1024
DEPTH = 4
GRID_W = 64
NA_HEADS = 8
NA_HEAD_DIM = 64
NA_WIDTH = NA_HEADS * NA_HEAD_DIM
NA_KH = 8
NA_KW = 16
NA_ROWS = 8
NA_WIN = NA_KH + NA_ROWS - 1
HG_HEADS = 4
HG_HEAD_DIM = 128
HG_WIDTH = HG_HEADS * HG_HEAD_DIM
HG_CHUNK = 64
N_EXPERTS = 16
EC_FACTOR = 2
D_FF = 2 * D_MODEL
IN_WIDTH = 3 * NA_WIDTH + 5 * HG_WIDTH + 2 * D_MODEL
ALPHA = (2 * DEPTH) ** 0.25
LN_EPS = 1e-5
RMS_EPS = 1e-6
NEG_BIG = -1e30
F_MIN = 1e-30

GATE_WIDTH = 2 * D_MODEL
HF_WIDTH = 3 * HG_WIDTH
HK_WIDTH = 4 * HG_WIDTH
MIX_SPLIT = 2
INPROJ_CHUNK = 256
HG_LEVELS = (32, 16, 8, 4, 2, 1)
HG_GROUP = 16
HG_TRI = 4
VMEM_LIMIT = 48 * 1024 * 1024
RANK_TILE = 256
RANK_STEP = 4
SLOTS_PER_PASS = 48
WIN = SLOTS_PER_PASS + 16
TAU_CHUNK = 2048
FILL_SPAN = 256


def _cparams(*sem):
    return pltpu.CompilerParams(dimension_semantics=sem, vmem_limit_bytes=VMEM_LIMIT)


def _sigmoid(x):
    return 0.5 * jnp.tanh(0.5 * x) + 0.5


def _layer_norm(x, g, b):
    mu = jnp.mean(x, axis=-1, keepdims=True)
    xc = x - mu
    var = jnp.mean(xc * xc, axis=-1, keepdims=True)
    return xc * lax.rsqrt(var + LN_EPS) * g + b


def _dot(a, b):
    return jnp.dot(a, b, preferred_element_type=F32)


def _dot_nt(a, b):
    return lax.dot_general(a, b, (((1,), (1,)), ((), ())), preferred_element_type=F32)


def _dot_tn(a, b):
    return lax.dot_general(a, b, (((0,), (0,)), ((), ())), preferred_element_type=F32)


def _inproj_kernel(*refs, norm):
    if norm:
        x_ref, w_ref, lbf_ref, lbb_ref, g_ref, b_ref, qkv_ref, gt_ref, hf_ref, hk_ref, xn_ref = refs
        x = _layer_norm(x_ref[...], g_ref[...], b_ref[...])
        xn_ref[...] = x
    else:
        x_ref, w_ref, lbf_ref, lbb_ref, qkv_ref, gt_ref, hf_ref, hk_ref = refs
        x = x_ref[...]
    xb = x.astype(BF16)
    cw = INPROJ_CHUNK
    for c in range(IN_WIDTH // cw):
        acc = _dot(xb, w_ref[:, c * cw:(c + 1) * cw])
        o = c * cw
        if o < 3 * NA_WIDTH:
            if o < NA_WIDTH:
                acc = acc * (NA_HEAD_DIM ** -0.5)
            qkv_ref[:, o:o + cw] = acc.astype(BF16)
        elif o < 3 * NA_WIDTH + 5 * HG_WIDTH:
            o -= 3 * NA_WIDTH
            s, o = o // HG_WIDTH, o % HG_WIDTH
            if s == 0:
                hf_ref[:, o:o + cw] = acc * _sigmoid(acc)
            elif s in (1, 2):
                lb = (lbf_ref if s == 1 else lbb_ref)[:, o:o + cw]
                sg = _sigmoid(acc)
                f = lb + (1.0 - lb) * sg
                hf_ref[:, s * HG_WIDTH + o:s * HG_WIDTH + o + cw] = jnp.log(jnp.maximum(f, F_MIN))
                hk_ref[:, (s - 1) * HG_WIDTH + o:(s - 1) * HG_WIDTH + o + cw] = (
                    (1.0 - lb) * (1.0 - sg)).astype(BF16)
            elif s == 3:
                hk_ref[:, 2 * HG_WIDTH + o:2 * HG_WIDTH + o + cw] = acc.astype(BF16)
            else:
                hk_ref[:, 3 * HG_WIDTH + o:3 * HG_WIDTH + o + cw] = (acc * _sigmoid(acc)).astype(BF16)
        else:
            o -= 3 * NA_WIDTH + 5 * HG_WIDTH
            gt_ref[:, o:o + cw] = _sigmoid(acc).astype(BF16)


def _inproj_call(x, w, lb_fwd, lb_bwd, ln=None, tm=512):
    n, d = x.shape
    vec = pl.BlockSpec((1, HG_WIDTH), lambda i: (0, 0))
    row = pl.BlockSpec((tm, d), lambda i: (i, 0))
    in_specs = [row, pl.BlockSpec((d, IN_WIDTH), lambda i: (0, 0), pipeline_mode=pl.Buffered(1)), vec, vec]
    args = [x, w, lb_fwd.reshape(1, HG_WIDTH), lb_bwd.reshape(1, HG_WIDTH)]
    out_specs = [pl.BlockSpec((tm, 3 * NA_WIDTH), lambda i: (i, 0)),
                 pl.BlockSpec((tm, GATE_WIDTH), lambda i: (i, 0)),
                 pl.BlockSpec((tm, HF_WIDTH), lambda i: (i, 0)),
                 pl.BlockSpec((tm, HK_WIDTH), lambda i: (i, 0))]
    out_shape = [jax.ShapeDtypeStruct((n, 3 * NA_WIDTH), BF16),
                 jax.ShapeDtypeStruct((n, GATE_WIDTH), BF16),
                 jax.ShapeDtypeStruct((n, HF_WIDTH), F32),
                 jax.ShapeDtypeStruct((n, HK_WIDTH), BF16)]
    if ln is not None:
        in_specs += [pl.BlockSpec((1, d), lambda i: (0, 0))] * 2
        args += [ln[0].reshape(1, d), ln[1].reshape(1, d)]
        out_specs.append(row)
        out_shape.append(jax.ShapeDtypeStruct((n, d), F32))
    return pl.pallas_call(
        functools.partial(_inproj_kernel, norm=ln is not None), grid=(n // tm,),
        in_specs=in_specs, out_specs=out_specs, out_shape=out_shape,
        compiler_params=_cparams("parallel"), name="in_proj")(*args)


def _na_kernel(*refs, rows):
    q_ref, k_ref, v_ref = refs[:3]
    bias_refs = refs[3:3 + NA_ROWS]
    o_ref, s_ref = refs[3 + NA_ROWS:]
    r0 = pl.program_id(1) * NA_ROWS
    w0 = jnp.minimum(_na_start(r0, rows), rows - NA_WIN)
    lane = lax.broadcasted_iota(jnp.int32, (1, 128), 1)
    lo = lane < NA_HEAD_DIM
    hi = jnp.logical_not(lo)
    offs = [pl.multiple_of((_na_start(r0 + i, rows) - w0) * GRID_W, GRID_W) for i in range(NA_ROWS)]
    for i in range(NA_ROWS):
        qr = slice(i * GRID_W, (i + 1) * GRID_W)
        kr = pl.ds(offs[i], NA_KH * GRID_W)
        for p in range(NA_HEADS // 2):
            cs = slice(p * 128, (p + 1) * 128)
            q2 = q_ref[qr, cs]
            k2 = k_ref[kr, cs]
            for hh in range(2):
                qm = jnp.where(hi if hh else lo, q2, jnp.zeros_like(q2))
                s = _dot_nt(qm, k2) + bias_refs[i][2 * p + hh]
                s_ref[i * NA_HEADS + 2 * p + hh] = s - jnp.max(s, axis=-1, keepdims=True)
    for i in range(NA_ROWS):
        qr = slice(i * GRID_W, (i + 1) * GRID_W)
        kr = pl.ds(offs[i], NA_KH * GRID_W)
        for p in range(NA_HEADS // 2):
            cs = slice(p * 128, (p + 1) * 128)
            v2 = v_ref[kr, cs]
            res = []
            for hh in range(2):
                e = jnp.exp(s_ref[i * NA_HEADS + 2 * p + hh]).astype(BF16)
                va = jnp.where(hi if hh else lo, v2, jnp.ones_like(v2))
                res.append(_dot(e, va))
            o = jnp.where(lo, res[0], res[1])
            l = pltpu.roll(jnp.where(lo, res[1], res[0]), NA_HEAD_DIM, 1)
            o_ref[qr, cs] = (o / l).astype(BF16)


def _na_start(r, rows):
    return jnp.clip(r - NA_KH // 2, 0, rows - NA_KH)


def _na_call(qkv, bias, batch, rows):
    n = qkv.shape[0]
    assert rows >= NA_WIN and rows % NA_ROWS == 0
    steps = rows // NA_ROWS

    def win_tok(b, j):
        w0 = jnp.minimum(_na_start(j * NA_ROWS, rows), rows - NA_WIN)
        return (b * rows + w0) * GRID_W

    def bias_spec(i):
        return pl.BlockSpec((None, NA_HEADS, GRID_W, NA_KH * GRID_W),
                            lambda b, j: (j * NA_ROWS + i - _na_start(j * NA_ROWS + i, rows), 0, 0, 0))

    win = (pl.Element(NA_WIN * GRID_W), pl.Element(NA_WIDTH))
    blk = (NA_ROWS * GRID_W, NA_WIDTH)
    return pl.pallas_call(
        functools.partial(_na_kernel, rows=rows), grid=(batch, steps),
        in_specs=[pl.BlockSpec(blk, lambda b, j: (b * steps + j, 0)),
                  pl.BlockSpec(win, lambda b, j: (win_tok(b, j), NA_WIDTH)),
                  pl.BlockSpec(win, lambda b, j: (win_tok(b, j), 2 * NA_WIDTH))]
                 + [bias_spec(i) for i in range(NA_ROWS)],
        out_specs=pl.BlockSpec(blk, lambda b, j: (b * steps + j, 0)),
        out_shape=jax.ShapeDtypeStruct((n, NA_WIDTH), BF16),
        scratch_shapes=[pltpu.VMEM((NA_ROWS * NA_HEADS, GRID_W, NA_KH * GRID_W), F32)],
        compiler_params=_cparams("parallel", "arbitrary"), name="na_attn")(
            qkv, qkv, qkv, *([bias] * NA_ROWS))


def _na_bias_tables(rpb):
    c = np.arange(GRID_W)
    c_start = np.clip(c - NA_KW // 2, 0, GRID_W - NA_KW)
    col_mask = (c[None, :] >= c_start[:, None]) & (c[None, :] < c_start[:, None] + NA_KW)
    dc = np.clip(c[None, :] - c[:, None], -(NA_KW - 1), NA_KW - 1) + (NA_KW - 1)
    onehot = jnp.asarray(dc[:, :, None] == np.arange(2 * NA_KW - 1), F32)
    t = jnp.einsum("lhab,qkb->lhqak", rpb.astype(F32), onehot, precision=lax.Precision.HIGHEST)
    t = jnp.where(col_mask[None, None, :, None, :], t, NEG_BIG)
    per_cls = [t[:, :, :, NA_KH - 1 - cls:2 * NA_KH - 1 - cls] for cls in range(NA_KH)]
    t = jnp.stack(per_cls, axis=1)
    return t.reshape(rpb.shape[0], NA_KH, NA_HEADS, GRID_W, NA_KH * GRID_W)


def _hg_constants(reverse):
    n = HG_CHUNK
    t = np.arange(n)
    if reverse:
        tri = (t[None, :] >= t[:, None])
    else:
        tri = (t[None, :] <= t[:, None])
    masks = []
    for c in HG_LEVELS:
        blk = t // (2 * c)
        upper = (t % (2 * c)) >= c
        same = blk[:, None] == blk[None, :]
        if reverse:
            m = same & (~upper)[:, None] & upper[None, :]
        else:
            m = same & upper[:, None] & (~upper)[None, :]
        masks.append(m)
    masks.append(np.eye(n, dtype=bool))
    tri = np.kron(np.eye(HG_TRI, dtype=bool), tri)
    return jnp.asarray(tri, BF16), jnp.asarray(np.stack(masks), F32)


def _hgrn_kernel(*refs, reverse, final):
    if final:
        q_ref, g_ref, k_ref, v_ref, tri_ref, lm_ref, gate_ref, prev_ref, ng_ref, o_ref, st_ref = refs
    else:
        q_ref, g_ref, k_ref, v_ref, tri_ref, lm_ref, o_ref, st_ref = refs
    n = HG_CHUNK
    grp = HG_GROUP
    w = HG_WIDTH
    m = grp * n

    @pl.when(pl.program_id(1) == 0)
    def _():
        st_ref[...] = jnp.zeros_like(st_ref)

    g = g_ref[...]
    kk = k_ref[...].astype(F32)
    qq = q_ref[...]
    v_b = v_ref[...]

    g1 = g.astype(BF16)
    r1 = g - g1.astype(F32)
    g2 = r1.astype(BF16)
    g3 = (r1 - g2.astype(F32)).astype(BF16)
    tri = tri_ref[...]
    tr = tri.shape[0]
    b = jnp.concatenate([_dot(tri, g1[r:r + tr]) + _dot(tri, g2[r:r + tr]) + _dot(tri, g3[r:r + tr])
                         for r in range(0, m, tr)], axis=0)
    last = 0 if reverse else n - 1
    bc = b.reshape(grp, n, w)
    b_last = bc[:, last:last + 1, :]
    e_last = jnp.exp(b_last)

    row = lax.broadcasted_iota(jnp.int32, (m, 1), 0)
    qs, ks = [], []
    for c in HG_LEVELS:
        ref_row = c if reverse else c - 1
        if c == 1:
            d = jnp.where((row % 2) == (0 if reverse else 1), g, 0.0)
        elif c == 2:
            b3 = b.reshape(m // 8, 8, w)
            sub = lax.broadcasted_iota(jnp.int32, (1, 8, 1), 1)
            r_lo = jnp.broadcast_to(b3[:, ref_row:ref_row + 1, :], b3.shape)
            r_hi = jnp.broadcast_to(b3[:, ref_row + 4:ref_row + 5, :], b3.shape)
            d = -jnp.abs(b3 - jnp.where(sub < 4, r_lo, r_hi)).reshape(m, w)
        else:
            b3 = b.reshape(m // (2 * c), 2 * c, w)
            d = -jnp.abs(b3 - jnp.broadcast_to(b3[:, ref_row:ref_row + 1, :], b3.shape)).reshape(m, w)
        e = jnp.exp(d)
        qs.append((qq * e).astype(BF16))
        ks.append((kk * e).astype(BF16))
    qs.append(qq.astype(BF16))
    ks.append(kk.astype(BF16))
    q_in = (qq * jnp.exp(b)).astype(BF16)
    k_out = (kk * jnp.exp(b_last - bc).reshape(m, w)).astype(BF16)

    intra = {}
    for ci in range(grp):
        rs = slice(ci * n, (ci + 1) * n)
        for h in range(HG_HEADS):
            hs = slice(h * HG_HEAD_DIM, (h + 1) * HG_HEAD_DIM)
            a = None
            for li in range(len(qs)):
                term = lm_ref[li] * _dot_nt(qs[li][rs, hs], ks[li][rs, hs])
                a = term if a is None else a + term
            intra[ci, h] = _dot(a.astype(BF16), v_b[rs, hs])

    if final:
        gs = gate_ref[...].astype(F32)
    for ci in (range(grp - 1, -1, -1) if reverse else range(grp)):
        rs = slice(ci * n, (ci + 1) * n)
        for h in range(HG_HEADS):
            hs = slice(h * HG_HEAD_DIM, (h + 1) * HG_HEAD_DIM)
            st = st_ref[h]
            o = intra[ci, h] + _dot_nt(q_in[rs, hs], st.astype(BF16))
            st_ref[h] = st * e_last[ci][:, hs] + _dot_tn(v_b[rs, hs], k_out[rs, hs])
            if final:
                o = o + prev_ref[rs, hs]
                o = o * lax.rsqrt(jnp.mean(o * o, axis=-1, keepdims=True) + RMS_EPS) * ng_ref[:, hs]
                o_ref[rs, hs] = (o * gs[rs, hs]).astype(BF16)
            else:
                o_ref[rs, hs] = o


def _hgrn_call(hf, hk, batch, seq, reverse, prev=None, norm_g=None):
    n = hf.shape[0]
    rows = HG_GROUP * HG_CHUNK
    assert seq % rows == 0
    nc = seq // rows
    final = prev is not None
    tri, masks = _hg_constants(reverse)

    def tok(b, c):
        return b * nc + ((nc - 1 - c) if reverse else c)

    def col(j):
        return pl.BlockSpec((rows, HG_WIDTH), lambda b, c: (tok(b, c), j))

    vec = pl.BlockSpec((1, HG_WIDTH), lambda b, c: (0, 0))
    d = 1 if reverse else 0
    in_specs = [col(0), col(1 + d), col(d), col(2),
                pl.BlockSpec(tri.shape, lambda b, c: (0, 0)),
                pl.BlockSpec(masks.shape, lambda b, c: (0, 0, 0))]
    args = [hf, hf, hk, hk, tri, masks]
    if final:
        in_specs += [col(3), col(0), vec]
        args += [hk, prev, norm_g.reshape(1, HG_WIDTH)]
    return pl.pallas_call(
        functools.partial(_hgrn_kernel, reverse=reverse, final=final),
        grid=(batch, nc), in_specs=in_specs,
        out_specs=pl.BlockSpec((rows, HG_WIDTH), lambda b, c: (tok(b, c), 0)),
        out_shape=jax.ShapeDtypeStruct((n, HG_WIDTH), BF16 if final else F32),
        scratch_shapes=[pltpu.VMEM((HG_HEADS, HG_HEAD_DIM, HG_HEAD_DIM), F32)],
        compiler_params=_cparams("arbitrary", "arbitrary"),
        name="hgrn_bwd" if reverse else "hgrn_fwd")(*args)


def _mix_kernel(na_ref, hg_ref, gt_ref, x_ref, wna_ref, whg_ref, wout_ref, g_ref, b_ref, wr_ref,
                x1_ref, x1b_ref, aff_ref):
    hm = x_ref.shape[0] // MIX_SPLIT
    for s in range(MIX_SPLIT):
        rs = slice(s * hm, (s + 1) * hm)
        a = _dot(na_ref[rs, :], wna_ref[...])
        r = _dot(hg_ref[rs, :], whg_ref[...])
        mix = gt_ref[rs, :D_MODEL].astype(F32) * a + gt_ref[rs, D_MODEL:].astype(F32) * r
        y = _dot(mix.astype(BF16), wout_ref[...])
        x1 = _layer_norm(ALPHA * x_ref[rs, :] + y, g_ref[...], b_ref[...])
        x1_ref[rs, :] = x1
        x_hi = x1.astype(BF16)
        x1b_ref[rs, :] = x_hi
        x_lo = (x1 - x_hi.astype(F32)).astype(BF16)
        p = _dot(x_hi, wr_ref[...]) + _dot(x_lo, wr_ref[...])
        logits = p[:, :N_EXPERTS] + p[:, N_EXPERTS:]
        e = jnp.exp(logits - jnp.max(logits, axis=-1, keepdims=True))
        aff_ref[rs, :] = e / jnp.sum(e, axis=-1, keepdims=True)


def _mix_call(na, hg, gt, x, wna, whg, wout, g, b, wr2, tm=512):
    n, d = x.shape
    const = lambda shape: pl.BlockSpec(shape, lambda i: (0,) * len(shape))
    return pl.pallas_call(
        _mix_kernel, grid=(n // tm,),
        in_specs=[pl.BlockSpec((tm, NA_WIDTH), lambda i: (i, 0)),
                  pl.BlockSpec((tm, HG_WIDTH), lambda i: (i, 0)),
                  pl.BlockSpec((tm, 2 * D_MODEL), lambda i: (i, 0)),
                  pl.BlockSpec((tm, d), lambda i: (i, 0)),
                  const((NA_WIDTH, d)), const((HG_WIDTH, d)), const((d, d)),
                  const((1, d)), const((1, d)), const((d, 2 * N_EXPERTS))],
        out_specs=[pl.BlockSpec((tm, d), lambda i: (i, 0)),
                   pl.BlockSpec((tm, d), lambda i: (i, 0)),
                   pl.BlockSpec((tm, N_EXPERTS), lambda i: (i, 0))],
        out_shape=[jax.ShapeDtypeStruct((n, d), F32), jax.ShapeDtypeStruct((n, d), BF16),
                   jax.ShapeDtypeStruct((n, N_EXPERTS), F32)],
        compiler_params=_cparams("parallel"), name="mix_router")(
            na, hg, gt, x, wna, whg, wout, g.reshape(1, d), b.reshape(1, d), wr2)


def _ffn_kernel(xs_ref, wg_ref, wu_ref, wd_ref, gate_ref, o_ref, acc_ref):
    k = pl.program_id(2)
    xs = xs_ref[...]
    hg = _dot(xs, wg_ref[...])
    hu = _dot(xs, wu_ref[...])
    part = _dot((hg * _sigmoid(hg) * hu).astype(BF16), wd_ref[...])

    @pl.when(k == 0)
    def _():
        acc_ref[...] = part

    @pl.when(k > 0)
    def _():
        acc_ref[...] += part

    @pl.when(k == pl.num_programs(2) - 1)
    def _():
        o_ref[...] = (acc_ref[...] * gate_ref[...]).astype(BF16)


def _ffn_call(xs, wg, wu, wd, layer, gate, tc=1024, tf=1024):
    e, c, d = xs.shape
    f = wg.shape[-1]
    tc = min(tc, c)
    return pl.pallas_call(
        _ffn_kernel, grid=(e, c // tc, f // tf),
        in_specs=[pl.BlockSpec((None, tc, d), lambda i, j, k: (i, j, 0)),
                  pl.BlockSpec((None, None, d, tf), lambda i, j, k: (layer, i, 0, k)),
                  pl.BlockSpec((None, None, d, tf), lambda i, j, k: (layer, i, 0, k)),
                  pl.BlockSpec((None, None, tf, d), lambda i, j, k: (layer, i, k, 0)),
                  pl.BlockSpec((None, tc, 1), lambda i, j, k: (i, j, 0))],
        out_specs=pl.BlockSpec((None, tc, d), lambda i, j, k: (i, j, 0)),
        out_shape=jax.ShapeDtypeStruct((e, c, d), BF16),
        scratch_shapes=[pltpu.VMEM((tc, d), F32)],
        compiler_params=_cparams("parallel", "parallel", "arbitrary"), name="expert_ffn")(
            xs, wg, wu, wd, gate.reshape(e, c, 1))


def _rank_kernel(aff_ref, tau_ref, need_ref, lrank_ref, base_ref, ceq_ref, csel_ref):
    @pl.when(pl.program_id(0) == 0)
    def _():
        ceq_ref[...] = jnp.zeros_like(ceq_ref)
        csel_ref[...] = jnp.zeros_like(csel_ref)

    tau = tau_ref[:, 0:1]
    u = lax.broadcasted_iota(jnp.int32, (RANK_TILE, RANK_TILE), 0)
    t = lax.broadcasted_iota(jnp.int32, (RANK_TILE, RANK_TILE), 1)
    before = jnp.where(u < t, 1.0, 0.0).astype(BF16)
    ones = jnp.ones((RANK_TILE, 128), BF16)
    for i in range(RANK_STEP):
        ts = slice(i * RANK_TILE, (i + 1) * RANK_TILE)
        a = aff_ref[:, ts]
        eq = a == tau
        eq_f = jnp.where(eq, 1.0, 0.0).astype(BF16)
        eq_before = ceq_ref[:, 0:1] + _dot(eq_f, before)
        sel = jnp.logical_or(a > tau, jnp.logical_and(eq, eq_before < need_ref[:, 0:1]))
        sel_f = jnp.where(sel, 1.0, 0.0).astype(BF16)
        lrank_ref[:, ts] = jnp.where(sel, _dot(sel_f, before), -1.0).astype(jnp.int32)
        base_ref[i] = csel_ref[...].astype(jnp.int32)
        ceq_ref[...] += _dot(eq_f, ones)
        csel_ref[...] += _dot(sel_f, ones)


def _rank_call(aff_t, tau, need):
    e, n = aff_t.shape
    nt = n // RANK_TILE
    step = RANK_STEP * RANK_TILE
    assert n % step == 0
    vec = pl.BlockSpec((e, 128), lambda j: (0, 0))
    return pl.pallas_call(
        _rank_kernel, grid=(n // step,),
        in_specs=[pl.BlockSpec((e, step), lambda j: (0, j)), vec, vec],
        out_specs=[pl.BlockSpec((e, step), lambda j: (0, j)),
                   pl.BlockSpec((RANK_STEP, e, 128), lambda j: (j, 0, 0))],
        out_shape=[jax.ShapeDtypeStruct((e, n), jnp.int32),
                   jax.ShapeDtypeStruct((nt, e, 128), jnp.int32)],
        scratch_shapes=[pltpu.VMEM((e, 128), F32), pltpu.VMEM((e, 128), F32)],
        compiler_params=_cparams("arbitrary"), name="slot_rank")(aff_t, tau, need)


def _tau_kernel(aff_ref, tau_ref, need_ref, *, cap):
    e, n = aff_ref.shape

    def count_ge(cand):
        def body(i, acc):
            c0 = pl.multiple_of(i * TAU_CHUNK, TAU_CHUNK)
            bits = pltpu.bitcast(aff_ref[:, pl.ds(c0, TAU_CHUNK)], jnp.int32)
            hit = jnp.where(bits >= cand, 1.0, 0.0)
            for k in range(TAU_CHUNK // 128):
                acc = acc + hit[:, k * 128:(k + 1) * 128]
            return acc
        acc = lax.fori_loop(0, n // TAU_CHUNK, body, jnp.zeros((e, 128), F32))
        return jnp.sum(acc, axis=1, keepdims=True)

    lo = jnp.zeros((e, 1), jnp.int32)
    for bit in range(30, -1, -1):
        cand = lo | (1 << bit)
        lo = jnp.where(count_ge(cand) >= cap, cand, lo)
    tau_ref[...] = jnp.broadcast_to(pltpu.bitcast(lo, F32), (e, 128))
    need_ref[...] = jnp.broadcast_to(cap - count_ge(lo + 1), (e, 128))


def _tau_call(aff_t, cap):
    e, n = aff_t.shape
    assert n % TAU_CHUNK == 0
    full = pl.BlockSpec((e, n), lambda i: (0, 0))
    vec = pl.BlockSpec((e, 128), lambda i: (0, 0))
    return pl.pallas_call(
        functools.partial(_tau_kernel, cap=cap), grid=(1,), in_specs=[full], out_specs=[vec, vec],
        out_shape=[jax.ShapeDtypeStruct((e, 128), F32)] * 2,
        compiler_params=_cparams("arbitrary"), name="route_tau")(aff_t)


def _fill_kernel(base_ref, next_ref, npass_ref, lrank_ref, idx_ref, *, cap):
    j = pl.program_id(0)
    nt = pl.num_programs(0)

    @pl.when(j == 0)
    def _():
        idx_ref[...] = jnp.zeros_like(idx_ref)

    tok = j * RANK_TILE + lax.broadcasted_iota(jnp.int32, (8, RANK_TILE), 1)
    sub = lax.broadcasted_iota(jnp.int32, (8, RANK_TILE), 0)
    vals = jnp.where(sub == 0, tok // 256, jnp.where(sub == 1, tok % 256, 0)).astype(F32).astype(BF16)
    row = lax.broadcasted_iota(jnp.int32, (WIN, RANK_TILE), 0)
    lane = lax.broadcasted_iota(jnp.int32, (1, FILL_SPAN), 1)

    def one_pass(p, carry):
        starts, sel = [], []
        for e in range(N_EXPERTS):
            s0 = base_ref[e * nt + j] + p * SLOTS_PER_PASS
            a = jnp.minimum((s0 // 16) * 16, cap - WIN)
            starts.append((s0, a))
            lr = lrank_ref[e:e + 1, :]
            pos = jnp.where(jnp.logical_and(lr >= p * SLOTS_PER_PASS, lr < (p + 1) * SLOTS_PER_PASS),
                            lr - p * SLOTS_PER_PASS + (s0 - a), -1)
            sel.append(jnp.where(row == pos, 1.0, 0.0).astype(BF16))
        got = _dot_nt(vals, jnp.concatenate(sel, axis=0))
        ids = (got[0:1, :] * 256.0 + got[1:2, :]).astype(jnp.int32)
        for e in range(N_EXPERTS):
            s0, a = starts[e]
            s1 = jnp.minimum(next_ref[e * nt + j], s0 + SLOTS_PER_PASS)
            a128 = pl.multiple_of((a // 128) * 128, 128)
            seg = jnp.concatenate([ids[:, e * WIN:(e + 1) * WIN],
                                   jnp.zeros((1, FILL_SPAN - WIN), jnp.int32)], axis=1)
            seg = pltpu.roll(seg, a - a128, 1)
            cur = idx_ref[e:e + 1, pl.ds(a128, FILL_SPAN)]
            keep = jnp.logical_and(lane >= s0 - a128, lane < s1 - a128)
            idx_ref[e:e + 1, pl.ds(a128, FILL_SPAN)] = jnp.where(keep, seg, cur)
        return carry

    lax.fori_loop(0, npass_ref[j], one_pass, 0)


def _fill_call(lrank, base, cap):
    e, n = lrank.shape
    nt = n // RANK_TILE
    nxt = jnp.concatenate([base[:, 1:], jnp.full((e, 1), cap, jnp.int32)], axis=1)
    npass = jnp.maximum(1, (jnp.max(nxt - base, axis=0) + SLOTS_PER_PASS - 1) // SLOTS_PER_PASS).astype(jnp.int32)
    grid_spec = pltpu.PrefetchScalarGridSpec(
        num_scalar_prefetch=3, grid=(nt,),
        in_specs=[pl.BlockSpec((e, RANK_TILE), lambda j, *_: (0, j))],
        out_specs=pl.BlockSpec((e, cap + FILL_SPAN), lambda j, *_: (0, 0)))
    idx = pl.pallas_call(
        functools.partial(_fill_kernel, cap=cap), grid_spec=grid_spec,
        out_shape=jax.ShapeDtypeStruct((e, cap + FILL_SPAN), jnp.int32),
        compiler_params=_cparams("arbitrary"), name="slot_fill")(
            base.reshape(-1), nxt.reshape(-1), npass, lrank)
    return idx[:, :cap]


def _combine_kernel(base_ref, npass_ref, x_ref, lrank_ref, g_ref, b_ref, out_hbm, o_ref,
                    buf_ref, acc_ref, sem_ref, *, cap):
    j = pl.program_id(0)
    nt = pl.num_programs(0)

    def win_start(tile, e, p):
        r = base_ref[e * nt + tile] + p * SLOTS_PER_PASS
        return jnp.minimum((r // 16) * 16, cap - WIN)

    def window_copy(tile, e, p, slot):
        a = pl.multiple_of(win_start(tile, e, p), 16)
        return pltpu.make_async_copy(out_hbm.at[e, pl.ds(a, WIN), :],
                                     buf_ref.at[slot, pl.ds(e * WIN, WIN), :], sem_ref.at[slot])

    def start_all(tile, p, slot):
        for e in range(N_EXPERTS):
            window_copy(tile, e, p, slot).start()

    def wait_all(tile, p, slot):
        for e in range(N_EXPERTS):
            window_copy(tile, e, p, slot).wait()

    def add_pass(p, slot):
        row = lax.broadcasted_iota(jnp.int32, (WIN, RANK_TILE), 0)
        sel = []
        for e in range(N_EXPERTS):
            lr = lrank_ref[e:e + 1, :]
            off = base_ref[e * nt + j] - win_start(j, e, p)
            pos = jnp.where(jnp.logical_and(lr >= p * SLOTS_PER_PASS, lr < (p + 1) * SLOTS_PER_PASS),
                            lr + off, -1)
            sel.append(jnp.where(row == pos, 1.0, 0.0).astype(BF16))
        acc_ref[...] += _dot_tn(jnp.concatenate(sel, axis=0), buf_ref[slot])

    slot = j % 2

    @pl.when(j == 0)
    def _():
        start_all(0, 0, 0)

    @pl.when(j + 1 < nt)
    def _():
        start_all(j + 1, 0, 1 - slot)

    acc_ref[...] = jnp.zeros_like(acc_ref)
    wait_all(j, 0, slot)
    add_pass(0, slot)

    def extra(p, carry):
        start_all(j, p, slot)
        wait_all(j, p, slot)
        add_pass(p, slot)
        return carry

    lax.fori_loop(1, npass_ref[j], extra, 0)
    o_ref[...] = _layer_norm(ALPHA * x_ref[...] + acc_ref[...], g_ref[...], b_ref[...])


def _combine_call(x1, lrank, base, out, g, b):
    n, d = x1.shape
    e, cap, _ = out.shape
    nt = n // RANK_TILE
    assert cap >= WIN and cap % 16 == 0
    cnt = jnp.concatenate([base[:, 1:], jnp.full((e, 1), cap, jnp.int32)], axis=1) - base
    npass = jnp.maximum(1, (jnp.max(cnt, axis=0) + SLOTS_PER_PASS - 1) // SLOTS_PER_PASS).astype(jnp.int32)
    vec = pl.BlockSpec((1, d), lambda j, *_: (0, 0))
    grid_spec = pltpu.PrefetchScalarGridSpec(
        num_scalar_prefetch=2, grid=(nt,),
        in_specs=[pl.BlockSpec((RANK_TILE, d), lambda j, *_: (j, 0)),
                  pl.BlockSpec((e, RANK_TILE), lambda j, *_: (0, j)),
                  vec, vec, pl.BlockSpec(memory_space=pl.ANY)],
        out_specs=pl.BlockSpec((RANK_TILE, d), lambda j, *_: (j, 0)),
        scratch_shapes=[pltpu.VMEM((2, e * WIN, d), BF16), pltpu.VMEM((RANK_TILE, d), F32),
                        pltpu.SemaphoreType.DMA((2,))])
    return pl.pallas_call(
        functools.partial(_combine_kernel, cap=cap), grid_spec=grid_spec,
        out_shape=jax.ShapeDtypeStruct((n, d), F32),
        compiler_params=_cparams("arbitrary"), name="combine_ln")(
            base.reshape(-1), npass, x1, lrank, g.reshape(1, d), b.reshape(1, d), out)


def _hi_lo(w):
    hi = w.astype(BF16)
    lo = (w - hi.astype(F32)).astype(BF16)
    return jnp.concatenate([hi, lo], axis=-1)


def _lower_bounds(p):
    sm = jax.nn.softmax(p.astype(F32), axis=0)
    return jnp.cumsum(sm, axis=0) - sm[0:1]


def _trunk(x, prm):
    batch, seq, d = x.shape
    n = batch * seq
    rows = seq // GRID_W
    cap = EC_FACTOR * n // N_EXPERTS
    x = x.reshape(n, d)
    for l in range(DEPTH):
        if l == 0:
            qkv, gt, hf, hk, x = _inproj_call(x, prm["w_in"][l], prm["lb_fwd"][l], prm["lb_bwd"][l],
                                              ln=(prm["ln_in_g"], prm["ln_in_b"]))
        else:
            qkv, gt, hf, hk = _inproj_call(x, prm["w_in"][l], prm["lb_fwd"][l], prm["lb_bwd"][l])
        na = _na_call(qkv, prm["na_bias"][l], batch, rows)
        o_f = _hgrn_call(hf, hk, batch, seq, reverse=False)
        hg = _hgrn_call(hf, hk, batch, seq, reverse=True, prev=o_f, norm_g=prm["hg_norm_g"][l])
        x1, x1b, aff = _mix_call(na, hg, gt, x, prm["w_branch_na"][l], prm["w_branch_hg"][l],
                                 prm["w_out"][l], prm["ln1_g"][l], prm["ln1_b"][l], prm["w_router2"][l])
        aff_t = aff.T
        tau, need = _tau_call(aff_t, cap)
        lrank, base = _rank_call(aff_t, tau, need)
        base = base[:, :, 0].T
        idx = _fill_call(lrank, base, cap)
        gate = jnp.take_along_axis(aff_t, idx, axis=1)
        xs = x1b[idx]
        out = _ffn_call(xs, prm["w_gate_e"], prm["w_up_e"], prm["w_down_e"], l, gate)
        x = _combine_call(x1, lrank, base, out, prm["ln2_g"][l], prm["ln2_b"][l])
    return x.reshape(batch, seq, d)


@jax.jit
def kernel(x_prompt, x_sample, ln_in_g, ln_in_b, w_in, na_rpb, hg_lb_fwd, hg_lb_bwd, hg_norm_g,
           w_branch_na, w_branch_hg, w_out, ln1_g, ln1_b, w_router, w_gate_e, w_up_e, w_down_e,
           ln2_g, ln2_b):
    prm = dict(
        ln_in_g=ln_in_g, ln_in_b=ln_in_b, w_in=w_in.astype(BF16), na_bias=_na_bias_tables(na_rpb),
        lb_fwd=_lower_bounds(hg_lb_fwd), lb_bwd=_lower_bounds(hg_lb_bwd), hg_norm_g=hg_norm_g,
        w_branch_na=w_branch_na.astype(BF16), w_branch_hg=w_branch_hg.astype(BF16),
        w_out=w_out.astype(BF16), ln1_g=ln1_g, ln1_b=ln1_b,
        w_router2=_hi_lo(w_router),
        w_gate_e=w_gate_e.astype(BF16), w_up_e=w_up_e.astype(BF16), w_down_e=w_down_e.astype(BF16),
        ln2_g=ln2_g, ln2_b=ln2_b)
    return _trunk(x_prompt, prm), _trunk(x_sample, prm)
```
